```python
import math
import jax, jax.numpy as jnp
from jax import lax
import numpy as np

D_MODEL = 2048
BATCH = 2
SEQ = 16384
DEPTH = 2

N_MIXERS = 2
N_FOX = (DEPTH + 1) // 2
N_RWKV = DEPTH // 2
FOX_HEAD_DIM = 64
FOX_HEADS = D_MODEL // FOX_HEAD_DIM
Q_BLOCK = 128
RWKV_HEAD_DIM = 64
RWKV_HEADS = D_MODEL // RWKV_HEAD_DIM
DECAY_LORA = max(32, int(round(1.8 * math.sqrt(D_MODEL) / 32)) * 32)
ICLR_LORA = max(32, int(round(1.8 * math.sqrt(D_MODEL) / 32)) * 32)
VRES_LORA = max(32, int(round(1.3 * math.sqrt(D_MODEL) / 32)) * 32)
N_SHIFT_MIX = 6
NORM_EPS = 1e-6
LNX_EPS = 64e-5

kernel_name = "fox_rwkv7_interleaved_adaln_trunk"


def rms_norm(x, w, eps=NORM_EPS):
    xf = x.astype(jnp.float32)
    y = xf * lax.rsqrt(jnp.mean(xf * xf, axis=-1, keepdims=True) + eps)
    return (y * w.astype(jnp.float32)).astype(x.dtype)


def fox_block_attention(q, k, v, cum):
    B, H, T, dh = q.shape
    scale = dh ** -0.5
    kpos = jnp.arange(T)

    def one_block(blk):
        start = blk * Q_BLOCK
        qb = lax.dynamic_slice_in_dim(q, start, Q_BLOCK, axis=2)
        cb = lax.dynamic_slice_in_dim(cum, start, Q_BLOCK, axis=2)
        s = jnp.einsum('bhqd,bhkd->bhqk', qb, k, preferred_element_type=jnp.float32) * scale
        s = s + cb[..., :, None] - cum[..., None, :]
        qpos = start + jnp.arange(Q_BLOCK)
        causal = kpos[None, :] <= qpos[:, None]
        s = jnp.where(causal, s, -jnp.inf)
        p = jax.nn.softmax(s, axis=-1)
        return jnp.einsum('bhqk,bhkd->bhqd', p.astype(v.dtype), v)

    out = lax.map(one_block, jnp.arange(T // Q_BLOCK))
    return jnp.transpose(out, (1, 2, 0, 3, 4)).reshape(B, H, T, dh)


def fox_branch(h, w_in, b_f, q_norm_w, k_norm_w, w_out):
    B, T, D = h.shape
    H, dh = FOX_HEADS, FOX_HEAD_DIM
    proj = h @ w_in
    q = proj[..., 0 * D:1 * D].reshape(B, T, H, dh)
    k = proj[..., 1 * D:2 * D].reshape(B, T, H, dh)
    v = proj[..., 2 * D:3 * D]
    z = proj[..., 3 * D:4 * D]
    f_logit = proj[..., 4 * D:] + b_f
    q = rms_norm(q, q_norm_w)
    k = rms_norm(k, k_norm_w)
    log_f = jax.nn.log_sigmoid(f_logit.astype(jnp.float32))
    cum = jnp.transpose(lax.cumsum(log_f, axis=1), (0, 2, 1))
    qh = jnp.transpose(q, (0, 2, 1, 3))
    kh = jnp.transpose(k, (0, 2, 1, 3))
    vh = jnp.transpose(v.reshape(B, T, H, dh), (0, 2, 1, 3))
    o = fox_block_attention(qh, kh, vh, cum)
    o = jnp.transpose(o, (0, 2, 1, 3)).reshape(B, T, D)
    return (o * jax.nn.silu(z)) @ w_out, v


def rwkv7_branch(h, v_first, mu, w_in, w0, w1, w2, a0, a1, a2, v0, v1, v2,
                 k_k, k_a, r_k, lnx_w, lnx_b, w_out):
    B, T, D = h.shape
    H, N = RWKV_HEADS, RWKV_HEAD_DIM
    f32 = jnp.float32
    xx = jnp.pad(h[:, :-1], ((0, 0), (1, 0), (0, 0))) - h
    xr = h + xx * mu[0]
    xw = h + xx * mu[1]
    xk = h + xx * mu[2]
    xv = h + xx * mu[3]
    xa = h + xx * mu[4]
    xg = h + xx * mu[5]
    r = xr @ w_in[0]
    k = xk @ w_in[1]
    v = xv @ w_in[2]
    z = xg @ w_in[3]
    w_log = -jax.nn.softplus(-(w0 + jnp.tanh(xw @ w1) @ w2).astype(f32)) - 0.5
    decay = jnp.exp(-jnp.exp(w_log))
    v = v + (v_first - v) * jax.nn.sigmoid(v0 + (xv @ v1) @ v2)
    a = jax.nn.sigmoid(a0 + (xa @ a1) @ a2)
    kk = (k * k_k).reshape(B, T, H, N).astype(f32)
    kk = kk / jnp.maximum(jnp.sqrt(jnp.sum(kk * kk, axis=-1, keepdims=True)), 1e-12)
    k = k * (1.0 + (a - 1.0) * k_a)

    def heads(t):
        return t.reshape(B, T, H, N).astype(f32)

    rh, wh, kh, vh, ah = heads(r), heads(decay), heads(k), heads(v), heads(a)
    a_vec = -kk
    b_vec = kk * ah

    def step(S, inp):
        r_t, w_t, k_t, v_t, a_t, b_t = inp
        sa = jnp.einsum('bhvk,bhk->bhv', S, a_t)
        S = S * w_t[:, :, None, :] + sa[..., None] * b_t[:, :, None, :] \
            + v_t[..., None] * k_t[:, :, None, :]
        y_t = jnp.einsum('bhvk,bhk->bhv', S, r_t)
        return S, y_t

    xs = tuple(jnp.moveaxis(t, 1, 0) for t in (rh, wh, kh, vh, a_vec, b_vec))
    S0 = jnp.zeros((B, H, N, N), f32)
    _, y = lax.scan(step, S0, xs)
    y = jnp.moveaxis(y, 0, 1)
    mean = jnp.mean(y, axis=-1, keepdims=True)
    var = jnp.mean(jnp.square(y - mean), axis=-1, keepdims=True)
    y = (y - mean) * lax.rsqrt(var + LNX_EPS)
    y = y * lnx_w.reshape(H, N).astype(f32) + lnx_b.reshape(H, N).astype(f32)
    y = y + jnp.sum(rh * kh * r_k.astype(f32), axis=-1, keepdims=True) * vh
    y = y.reshape(B, T, D).astype(h.dtype)
    return (y * jax.nn.silu(z)) @ w_out


def setup_inputs(seed: int = 0) -> dict:
    key = jax.random.key(seed)
    ks = iter(jax.random.split(key, 40))
    D, H = D_MODEL, FOX_HEADS

    def nrm(shape, scale):
        return jax.random.normal(next(ks), shape, jnp.float32) * scale

    def uni(shape, lo, hi):
        return jax.random.uniform(next(ks), shape, jnp.float32, minval=lo, maxval=hi)

    return {
        "x": nrm((BATCH, SEQ, D), 1.0),
        "c": nrm((BATCH, D), 1.0),
        "norm_w": 1.0 + nrm((DEPTH, D), 0.02),
        "w_mod": nrm((DEPTH, D, 3 * D), D ** -0.5),
        "b_mod": nrm((DEPTH, 3 * D), 0.02),
        "fox_w_in": nrm((N_FOX, D, 4 * D + H), D ** -0.5),
        "fox_b_f": uni((N_FOX, H), 1.0, 4.0),
        "fox_q_norm_w": 1.0 + nrm((N_FOX, FOX_HEAD_DIM), 0.02),
        "fox_k_norm_w": 1.0 + nrm((N_FOX, FOX_HEAD_DIM), 0.02),
        "fox_w_out": nrm((N_FOX, D, D), D ** -0.5),
        "rwkv_mu": uni((N_RWKV, N_SHIFT_MIX, D), 0.0, 1.0),
        "rwkv_w_in": nrm((N_RWKV, 4, D, D), D ** -0.5),
        "rwkv_w0": uni((N_RWKV, D), -6.0, -1.0),
        "rwkv_w1": nrm((N_RWKV, D, DECAY_LORA), D ** -0.5),
        "rwkv_w2": nrm((N_RWKV, DECAY_LORA, D), 0.5 * DECAY_LORA ** -0.5),
        "rwkv_a0": nrm((N_RWKV, D), 0.1),
        "rwkv_a1": nrm((N_RWKV, D, ICLR_LORA), D ** -0.5),
        "rwkv_a2": nrm((N_RWKV, ICLR_LORA, D), 0.5 * ICLR_LORA ** -0.5),
        "rwkv_v0": nrm((N_RWKV, D), 0.1),
        "rwkv_v1": nrm((N_RWKV, D, VRES_LORA), D ** -0.5),
        "rwkv_v2": nrm((N_RWKV, VRES_LORA, D), 0.5 * VRES_LORA ** -0.5),
        "rwkv_k_k": 0.85 + nrm((N_RWKV, D), 0.02),
        "rwkv_k_a": 1.0 + nrm((N_RWKV, D), 0.02),
        "rwkv_r_k": nrm((N_RWKV, RWKV_HEADS, RWKV_HEAD_DIM), 0.1),
        "rwkv_lnx_w": 1.0 + nrm((N_RWKV, D), 0.02),
        "rwkv_lnx_b": nrm((N_RWKV, D), 0.02),
        "rwkv_w_out": nrm((N_RWKV, D, D), D ** -0.5),
    }


def reference(x, c, norm_w, w_mod, b_mod, fox_w_in, fox_b_f, fox_q_norm_w, fox_k_norm_w,
              fox_w_out, rwkv_mu, rwkv_w_in, rwkv_w0, rwkv_w1, rwkv_w2, rwkv_a0, rwkv_a1,
              rwkv_a2, rwkv_v0, rwkv_v1, rwkv_v2, rwkv_k_k, rwkv_k_a, rwkv_r_k, rwkv_lnx_w,
              rwkv_lnx_b, rwkv_w_out):
    c_act = jax.nn.silu(c)
    v_first = None
    for i in range(DEPTH):
        mod = c_act @ w_mod[i] + b_mod[i]
        shift, scale, gate = jnp.split(mod, 3, axis=-1)
        h = rms_norm(x, norm_w[i]) * (1.0 + scale[:, None, :]) + shift[:, None, :]
        j = i // N_MIXERS
        if i % N_MIXERS == 0:
            out, v = fox_branch(h, fox_w_in[j], fox_b_f[j], fox_q_norm_w[j],
                                fox_k_norm_w[j], fox_w_out[j])
            if v_first is None:
                v_first = v
        else:
            out = rwkv7_branch(h, v_first, rwkv_mu[j], rwkv_w_in[j], rwkv_w0[j], rwkv_w1[j],
                               rwkv_w2[j], rwkv_a0[j], rwkv_a1[j], rwkv_a2[j], rwkv_v0[j],
                               rwkv_v1[j], rwkv_v2[j], rwkv_k_k[j], rwkv_k_a[j], rwkv_r_k[j],
                               rwkv_lnx_w[j], rwkv_lnx_b[j], rwkv_w_out[j])
        x = x + gate[:, None, :] * out
    return x
```

```python
import functools

import jax
import jax.numpy as jnp
from jax import lax
from jax.experimental import pallas as pl
from jax.experimental.pallas import tpu as pltpu

F32 = jnp.float32
BF16 = jnp.bfloat16

HEAD_DIM = 64
LANES = 128
MXU_DIM = 256
RWKV_CHUNK = 64
HEADS_PER_GROUP = MXU_DIM // HEAD_DIM
NORM_EPS = 1e-6
LNX_EPS = 64e-5
VMEM_LIMIT = 56 * 1024 * 1024
NEG_BIG = -1e30


def _tile(n, pref):
    t = min(n, pref)
    while n % t:
        t //= 2
    return t


def _norm_mod_kernel(x_ref, nw_ref, sc_ref, sh_ref, o_ref):
    x = x_ref[...]
    ms = jnp.mean(x * x, axis=-1, keepdims=True)
    y = x * lax.rsqrt(ms + NORM_EPS) * nw_ref[...]
    o_ref[...] = y * (1.0 + sc_ref[...]) + sh_ref[...]


def _norm_mod(x, norm_w, scale, shift):
    B, T, D = x.shape
    tm = _tile(T, 512)
    row = pl.BlockSpec((None, tm, D), lambda b, i: (b, i, 0))
    per_b = pl.BlockSpec((None, 1, D), lambda b, i: (b, 0, 0))
    return pl.pallas_call(
        _norm_mod_kernel,
        grid=(B, T // tm),
        in_specs=[row, pl.BlockSpec((1, D), lambda b, i: (0, 0)), per_b, per_b],
        out_specs=row,
        out_shape=jax.ShapeDtypeStruct((B, T, D), F32),
        compiler_params=pltpu.CompilerParams(
            dimension_semantics=("parallel", "parallel"), vmem_limit_bytes=VMEM_LIMIT),
        name="norm_mod",
    )(x, norm_w.reshape(1, D), scale.reshape(B, 1, D), shift.reshape(B, 1, D))


def _matmul_kernel(a_ref, b_ref, o_ref):
    o_ref[...] = jnp.dot(a_ref[...], b_ref[...], preferred_element_type=F32).astype(o_ref.dtype)


def _matmul(a, b, out_dtype=F32):
    M, K = a.shape
    N = b.shape[1]
    tm, tn = _tile(M, 1024), _tile(N, 512)
    return pl.pallas_call(
        _matmul_kernel,
        grid=(M // tm, N // tn),
        in_specs=[pl.BlockSpec((tm, K), lambda i, j: (i, 0)),
                  pl.BlockSpec((K, tn), lambda i, j: (0, j))],
        out_specs=pl.BlockSpec((tm, tn), lambda i, j: (i, j)),
        out_shape=jax.ShapeDtypeStruct((M, N), out_dtype),
        compiler_params=pltpu.CompilerParams(
            dimension_semantics=("parallel", "parallel"), vmem_limit_bytes=VMEM_LIMIT),
        name="matmul",
    )(a, b)


def _matmul_residual_kernel(a_ref, b_ref, x_ref, g_ref, o_ref):
    acc = jnp.dot(a_ref[...], b_ref[...], preferred_element_type=F32)
    o_ref[...] = x_ref[...] + g_ref[...] * acc


def _matmul_residual(a, b, x, gate):
    B, T, K = a.shape
    N = b.shape[1]
    tm, tn = _tile(T, 1024), _tile(N, 512)
    return pl.pallas_call(
        _matmul_residual_kernel,
        grid=(B, T // tm, N // tn),
        in_specs=[pl.BlockSpec((None, tm, K), lambda bb, i, j: (bb, i, 0)),
                  pl.BlockSpec((K, tn), lambda bb, i, j: (0, j)),
                  pl.BlockSpec((None, tm, tn), lambda bb, i, j: (bb, i, j)),
                  pl.BlockSpec((None, 1, tn), lambda bb, i, j: (bb, 0, j))],
        out_specs=pl.BlockSpec((None, tm, tn), lambda bb, i, j: (bb, i, j)),
        out_shape=jax.ShapeDtypeStruct((B, T, N), F32),
        compiler_params=pltpu.CompilerParams(
            dimension_semantics=("parallel", "parallel", "parallel"), vmem_limit_bytes=VMEM_LIMIT),
        name="matmul_residual",
    )(a, b, x, gate.reshape(B, 1, N))


def _mod_kernel(c_ref, w_ref, b_ref, o_ref):
    c = c_ref[...]
    c_act = c * jax.nn.sigmoid(c)
    hi = c_act.astype(BF16)
    lo = (c_act - hi.astype(F32)).astype(BF16)
    w = w_ref[...]
    w_hi = w.astype(BF16)
    w_lo = (w - w_hi.astype(F32)).astype(BF16)
    acc = (jnp.dot(lo, w_hi, preferred_element_type=F32) + jnp.dot(hi, w_lo, preferred_element_type=F32)
           + jnp.dot(hi, w_hi, preferred_element_type=F32))
    o_ref[...] = acc + b_ref[...]


def _adaln_mod(c_pad, w_mod, b_mod):
    R, D = c_pad.shape
    N = w_mod.shape[1]
    tn = _tile(N, 512)
    return pl.pallas_call(
        _mod_kernel,
        grid=(N // tn,),
        in_specs=[pl.BlockSpec((R, D), lambda j: (0, 0)),
                  pl.BlockSpec((D, tn), lambda j: (0, j)),
                  pl.BlockSpec((1, tn), lambda j: (0, j))],
        out_specs=pl.BlockSpec((R, tn), lambda j: (0, j)),
        out_shape=jax.ShapeDtypeStruct((R, N), F32),
        compiler_params=pltpu.CompilerParams(
            dimension_semantics=("parallel",), vmem_limit_bytes=VMEM_LIMIT),
        name="adaln_mod",
    )(c_pad, w_mod, b_mod.reshape(1, N))


def _fox_attn_kernel(q_ref, k_ref, v_ref, o_ref, *, tq):
    qi = pl.program_id(2)
    row = lax.broadcasted_iota(jnp.int32, (tq, tq), 0)
    col = lax.broadcasted_iota(jnp.int32, (tq, tq), 1)
    causal = col <= row
    nt = (((1,), (1,)), ((), ()))
    outs = []
    for h in range(2):
        q = q_ref[h]

        def step(j, carry, masked):
            m, l, acc = carry
            start = pl.multiple_of(j * tq, tq)
            kj = k_ref[h, pl.ds(start, tq), :]
            vj = v_ref[pl.ds(start, tq), :]
            s = lax.dot_general(q, kj, nt, preferred_element_type=F32)
            if masked:
                s = jnp.where(causal, s, NEG_BIG)
            m_new = jnp.maximum(m, jnp.max(s, axis=1, keepdims=True))
            alpha = jnp.exp(m - m_new)
            p = jnp.exp(s - m_new)
            l = alpha * l + jnp.sum(p, axis=1, keepdims=True)
            acc = alpha * acc + jnp.dot(p.astype(BF16), vj, preferred_element_type=F32)
            return m_new, l, acc

        init = (jnp.full((tq, 1), NEG_BIG, F32), jnp.zeros((tq, 1), F32), jnp.zeros((tq, LANES), F32))
        carry = lax.fori_loop(0, qi, functools.partial(step, masked=False), init)
        m, l, acc = step(qi, carry, True)
        outs.append(acc / l)
    lane = lax.broadcasted_iota(jnp.int32, (tq, LANES), 1)
    o_ref[...] = jnp.where(lane < HEAD_DIM, outs[0], outs[1])


def _fox_attention(q_aug, k_aug, v):
    B, H, T, _ = q_aug.shape
    D = v.shape[-1]
    tq = _tile(T, 512)
    return pl.pallas_call(
        functools.partial(_fox_attn_kernel, tq=tq),
        grid=(B, H // 2, T // tq),
        in_specs=[pl.BlockSpec((None, 2, tq, LANES), lambda b, hp, i: (b, hp, i, 0)),
                  pl.BlockSpec((None, 2, T, LANES), lambda b, hp, i: (b, hp, 0, 0)),
                  pl.BlockSpec((None, T, LANES), lambda b, hp, i: (b, 0, hp))],
        out_specs=pl.BlockSpec((None, tq, LANES), lambda b, hp, i: (b, i, hp)),
        out_shape=jax.ShapeDtypeStruct((B, T, D), F32),
        compiler_params=pltpu.CompilerParams(
            dimension_semantics=("parallel", "parallel", "arbitrary"), vmem_limit_bytes=VMEM_LIMIT),
        name="fox_attention",
    )(q_aug, k_aug, v)


def _trunc_bf16(x):
    bits = lax.bitcast_convert_type(x, jnp.uint32) & jnp.uint32(0xFFFF0000)
    return lax.bitcast_convert_type(bits, F32)


def _split3(x):
    hi = _trunc_bf16(x)
    r1 = x - hi
    mid = _trunc_bf16(r1)
    lo = r1 - mid
    return hi.astype(BF16), mid.astype(BF16), lo.astype(BF16)


_NN = (((1,), (0,)), ((), ()))
_NT = (((1,), (1,)), ((), ()))
_TN = (((0,), (0,)), ((), ()))


def _split2(x):
    hi = x.astype(BF16)
    return hi, (x - hi.astype(F32)).astype(BF16)


def _dot3(a, b, dims=_NN):
    a_hi, a_lo = _split2(a)
    b_hi, b_lo = _split2(b)
    dg = functools.partial(lax.dot_general, dimension_numbers=dims, preferred_element_type=F32)
    return (dg(a_lo, b_hi) + dg(a_hi, b_lo)) + dg(a_hi, b_hi)


def _rwkv_chunk(r, lw, k, v, a, b, S):
    C, W = r.shape
    t_idx = lax.broadcasted_iota(jnp.int32, (C, W), 0)
    lane = lax.broadcasted_iota(jnp.int32, (C, W), 1)
    i_idx = lane & (HEAD_DIM - 1)
    lane_head = lane >> 6
    strict = i_idx < t_idx
    incl = i_idx <= t_idx
    n_heads = W // HEAD_DIM

    def bdrows(x):
        return jnp.concatenate([jnp.where(lane_head == h, x, 0.0) for h in range(n_heads)], axis=0)

    rr = lax.broadcasted_iota(jnp.int32, (C, C), 0)
    cc = lax.broadcasted_iota(jnp.int32, (C, C), 1)
    tri = jnp.where(cc <= rr, 1.0, 0.0).astype(BF16)
    lw_hi, lw_mid, lw_lo = _split3(lw)
    mm = functools.partial(jnp.dot, preferred_element_type=F32)
    cum = (mm(tri, lw_lo) + mm(tri, lw_mid)) + mm(tri, lw_hi)
    cum_end = cum[C - 1:C, :]
    e_pos = jnp.exp(cum)
    e_neg = jnp.exp(-cum)
    e_end = jnp.exp(cum_end - cum)
    At = a * jnp.exp(cum - lw)
    Rt = r * e_pos
    Bt = b * e_neg
    Kt = k * e_neg
    Bh = b * e_end
    Kh = k * e_end
    wc = jnp.exp(cum_end)

    AR = jnp.concatenate([At, Rt], axis=0)
    sB = _dot3(AR, bdrows(Bt), _NT)
    sK = _dot3(AR, bdrows(Kt), _NT)
    A_ab = jnp.where(strict, sB[:C], 0.0)
    A_rb = jnp.where(incl, sB[C:], 0.0)
    A_ak = jnp.where(strict, sK[:C], 0.0)
    A_rk = jnp.where(incl, sK[C:], 0.0)

    P = jnp.where(i_idx == t_idx, 1.0, 0.0) + A_ab
    Apow = A_ab
    n_sq = max(C.bit_length() - 2, 0)
    for _ in range(n_sq):
        Apow = _dot3(Apow, bdrows(Apow))
        P = P + _dot3(Apow, bdrows(P))

    vbd = bdrows(v)
    X = _dot3(At, S, _NT) + _dot3(A_ak, vbd)
    U = _dot3(P, bdrows(X))
    Y = _dot3(Rt, S, _NT) + _dot3(A_rb, bdrows(U)) + _dot3(A_rk, vbd)
    rw = lax.broadcasted_iota(jnp.int32, (W, W), 0) >> 6
    cw = lax.broadcasted_iota(jnp.int32, (W, W), 1) >> 6
    upd = _dot3(jnp.concatenate([U, v], axis=0), jnp.concatenate([Bh, Kh], axis=0), _TN)
    S_new = S * wc + jnp.where(rw == cw, upd, 0.0)
    return Y, S_new


def _rwkv_kernel(r_ref, lw_ref, k_ref, v_ref, a_ref, b_ref, y_ref, s_ref):
    @pl.when(pl.program_id(2) == 0)
    def _():
        s_ref[...] = jnp.zeros_like(s_ref)

    y, s_new = _rwkv_chunk(r_ref[...], lw_ref[...], k_ref[...], v_ref[...], a_ref[...], b_ref[...],
                           s_ref[...])
    y_ref[...] = y
    s_ref[...] = s_new


def _rwkv_recurrence(r, lw, k, v, a, b):
    B, T, D = r.shape
    C, W = RWKV_CHUNK, MXU_DIM
    spec = pl.BlockSpec((None, C, W), lambda bb, g, c: (bb, c, g))
    return pl.pallas_call(
        _rwkv_kernel,
        grid=(B, D // W, T // C),
        in_specs=[spec] * 6,
        out_specs=spec,
        out_shape=jax.ShapeDtypeStruct((B, T, D), F32),
        scratch_shapes=[pltpu.VMEM((W, W), F32)],
        compiler_params=pltpu.CompilerParams(
            dimension_semantics=("parallel", "parallel", "arbitrary"), vmem_limit_bytes=VMEM_LIMIT),
        name="rwkv7_recurrence",
    )(r, lw, k, v, a, b)


def _head_rms(x, w):
    y = x * lax.rsqrt(jnp.mean(x * x, axis=-1, keepdims=True) + NORM_EPS)
    return y * w


def _fox_layer(x, h, gate, w_in, b_f, q_norm_w, k_norm_w, w_out):
    B, T, D = x.shape
    H, dh = D // HEAD_DIM, HEAD_DIM
    M = B * T
    n_f = w_in.shape[1] - 4 * D
    w_pad = jnp.pad(w_in, ((0, 0), (0, LANES - n_f))).astype(BF16)
    proj = _matmul(h.reshape(M, D).astype(BF16), w_pad).reshape(B, T, 4 * D + LANES)
    q = _head_rms(proj[..., :D].reshape(B, T, H, dh), q_norm_w) * (dh ** -0.5)
    k = _head_rms(proj[..., D:2 * D].reshape(B, T, H, dh), k_norm_w)
    v = proj[..., 2 * D:3 * D]
    z = proj[..., 3 * D:4 * D]
    f_logit = proj[..., 4 * D:4 * D + n_f] + b_f
    cum = jnp.cumsum(jax.nn.log_sigmoid(f_logit), axis=1)
    c_hi, c_mid, c_lo = _split3(cum)
    ones = jnp.ones_like(c_hi)
    pad = jnp.zeros((B, T, H, LANES - dh - 6), BF16)
    q_aug = jnp.concatenate(
        [q.astype(BF16)] + [t[..., None] for t in (c_hi, c_mid, c_lo, ones, ones, ones)] + [pad], axis=-1)
    k_aug = jnp.concatenate(
        [k.astype(BF16)] + [t[..., None] for t in (ones, ones, ones, -c_hi, -c_mid, -c_lo)] + [pad], axis=-1)
    q_aug = jnp.transpose(q_aug, (0, 2, 1, 3))
    k_aug = jnp.transpose(k_aug, (0, 2, 1, 3))
    o = _fox_attention(q_aug, k_aug, v.astype(BF16))
    g = (o * jax.nn.silu(z)).astype(BF16)
    return _matmul_residual(g, w_out.astype(BF16), x, gate), v


def _pad_cols(w, n):
    return jnp.pad(w, ((0, 0), (0, n - w.shape[1])))


def _pad_rows(w, n):
    return jnp.pad(w, ((0, n - w.shape[0]), (0, 0)))


def _lora(xb, w1, w2, act):
    rank = w1.shape[1]
    rp = -(-rank // LANES) * LANES
    mid = act(_matmul(xb, _pad_cols(w1, rp).astype(BF16)))
    return _matmul(mid.astype(BF16), _pad_rows(w2, rp).astype(BF16))


def _rwkv_layer(x, h, gate, v_first, mu, w_in, w0, w1, w2, a0, a1, a2, v0, v1, v2,
                k_k, k_a, r_k, lnx_w, lnx_b, w_out):
    B, T, D = x.shape
    H, N = D // HEAD_DIM, HEAD_DIM
    M = B * T
    xx = jnp.pad(h[:, :-1], ((0, 0), (1, 0), (0, 0))) - h
    xr, xw, xk, xv, xa, xg = [(h + xx * mu[i]).reshape(M, D).astype(BF16) for i in range(6)]
    r = _matmul(xr, w_in[0].astype(BF16))
    k = _matmul(xk, w_in[1].astype(BF16))
    v = _matmul(xv, w_in[2].astype(BF16))
    z = _matmul(xg, w_in[3].astype(BF16))
    w_log = -jax.nn.softplus(-(w0 + _lora(xw, w1, w2, jnp.tanh))) - 0.5
    lw = -jnp.exp(w_log)
    v = v + (v_first.reshape(M, D) - v) * jax.nn.sigmoid(v0 + _lora(xv, v1, v2, lambda t: t))
    a = jax.nn.sigmoid(a0 + _lora(xa, a1, a2, lambda t: t))
    kk = (k * k_k).reshape(M, H, N)
    kk = (kk / jnp.maximum(jnp.sqrt(jnp.sum(kk * kk, axis=-1, keepdims=True)), 1e-12)).reshape(M, D)
    k = k * (1.0 + (a - 1.0) * k_a)
    shp = (B, T, D)
    y = _rwkv_recurrence(r.reshape(shp), lw.reshape(shp), k.reshape(shp), v.reshape(shp),
                         (-kk).reshape(shp), (kk * a).reshape(shp))
    y = y.reshape(M, H, N)
    mean = jnp.mean(y, axis=-1, keepdims=True)
    var = jnp.mean(jnp.square(y - mean), axis=-1, keepdims=True)
    y = (y - mean) * lax.rsqrt(var + LNX_EPS)
    y = y * lnx_w.reshape(H, N) + lnx_b.reshape(H, N)
    rh, kh, vh = r.reshape(M, H, N), k.reshape(M, H, N), v.reshape(M, H, N)
    y = y + jnp.sum(rh * kh * r_k, axis=-1, keepdims=True) * vh
    g = (y.reshape(M, D) * jax.nn.silu(z)).astype(BF16).reshape(B, T, D)
    return _matmul_residual(g, w_out.astype(BF16), x, gate)


def kernel(x, c, norm_w, w_mod, b_mod, fox_w_in, fox_b_f, fox_q_norm_w, fox_k_norm_w, fox_w_out, rwkv_mu, rwkv_w_in, rwkv_w0, rwkv_w1, rwkv_w2, rwkv_a0, rwkv_a1, rwkv_a2, rwkv_v0, rwkv_v1, rwkv_v2, rwkv_k_k, rwkv_k_a, rwkv_r_k, rwkv_lnx_w, rwkv_lnx_b, rwkv_w_out):
    B, T, D = x.shape
    depth = norm_w.shape[0]
    c_pad = jnp.pad(c, ((0, 8 - B % 8 if B % 8 else 0), (0, 0)))
    v_first = None
    for i in range(depth):
        mod = _adaln_mod(c_pad, w_mod[i], b_mod[i])[:B]
        shift, scale, gate = mod[:, :D], mod[:, D:2 * D], mod[:, 2 * D:]
        h = _norm_mod(x, norm_w[i], scale, shift)
        j = i // 2
        if i % 2 == 0:
            x, v = _fox_layer(x, h, gate, fox_w_in[j], fox_b_f[j], fox_q_norm_w[j], fox_k_norm_w[j],
                              fox_w_out[j])
            if v_first is None:
                v_first = v
        else:
            x = _rwkv_layer(x, h, gate, v_first, rwkv_mu[j], rwkv_w_in[j], rwkv_w0[j], rwkv_w1[j],
                            rwkv_w2[j], rwkv_a0[j], rwkv_a1[j], rwkv_a2[j], rwkv_v0[j], rwkv_v1[j],
                            rwkv_v2[j], rwkv_k_k[j], rwkv_k_a[j], rwkv_r_k[j], rwkv_lnx_w[j],
                            rwkv_lnx_b[j], rwkv_w_out[j])
    return x
```

```python
import functools

import jax
import jax.numpy as jnp
from jax import lax
from jax.experimental import pallas as pl
from jax.experimental.pallas import tpu as pltpu

F32 = jnp.float32
BF16 = jnp.bfloat16

HEAD_DIM = 64
LANES = 128
MXU_DIM = 256
RWKV_CHUNK = 64
HEADS_PER_GROUP = MXU_DIM // HEAD_DIM
NORM_EPS = 1e-6
LNX_EPS = 64e-5
VMEM_LIMIT = 56 * 1024 * 1024
NEG_BIG = -1e30


def _tile(n, pref):
    t = min(n, pref)
    while n % t:
        t //= 2
    return t


def _norm_mod_kernel(x_ref, nw_ref, sc_ref, sh_ref, o_ref):
    x = x_ref[...]
    ms = jnp.mean(x * x, axis=-1, keepdims=True)
    y = x * lax.rsqrt(ms + NORM_EPS) * nw_ref[...]
    o_ref[...] = (y * (1.0 + sc_ref[...]) + sh_ref[...]).astype(o_ref.dtype)


def _norm_mod(x, norm_w, scale, shift, out_dtype):
    B, T, D = x.shape
    tm = _tile(T, 512)
    row = pl.BlockSpec((None, tm, D), lambda b, i: (b, i, 0))
    per_b = pl.BlockSpec((None, 1, D), lambda b, i: (b, 0, 0))
    return pl.pallas_call(
        _norm_mod_kernel,
        grid=(B, T // tm),
        in_specs=[row, pl.BlockSpec((1, D), lambda b, i: (0, 0)), per_b, per_b],
        out_specs=row,
        out_shape=jax.ShapeDtypeStruct((B, T, D), out_dtype),
        compiler_params=pltpu.CompilerParams(
            dimension_semantics=("parallel", "parallel"), vmem_limit_bytes=VMEM_LIMIT),
        name="norm_mod",
    )(x, norm_w.reshape(1, D), scale.reshape(B, 1, D), shift.reshape(B, 1, D))


def _matmul_kernel(a_ref, b_ref, o_ref):
    o_ref[...] = jnp.dot(a_ref[...], b_ref[...], preferred_element_type=F32).astype(o_ref.dtype)


def _matmul(a, b, out_dtype=F32):
    M, K = a.shape
    N = b.shape[1]
    tm, tn = _tile(M, 1024), _tile(N, 512)
    return pl.pallas_call(
        _matmul_kernel,
        grid=(M // tm, N // tn),
        in_specs=[pl.BlockSpec((tm, K), lambda i, j: (i, 0)),
                  pl.BlockSpec((K, tn), lambda i, j: (0, j))],
        out_specs=pl.BlockSpec((tm, tn), lambda i, j: (i, j)),
        out_shape=jax.ShapeDtypeStruct((M, N), out_dtype),
        compiler_params=pltpu.CompilerParams(
            dimension_semantics=("parallel", "parallel"), vmem_limit_bytes=VMEM_LIMIT),
        name="matmul",
    )(a, b)


def _matmul_residual_kernel(a_ref, b_ref, x_ref, g_ref, o_ref):
    acc = jnp.dot(a_ref[...], b_ref[...], preferred_element_type=F32)
    o_ref[...] = x_ref[...] + g_ref[...] * acc


def _matmul_residual(a, b, x, gate):
    B, T, K = a.shape
    N = b.shape[1]
    tm, tn = _tile(T, 1024), _tile(N, 512)
    return pl.pallas_call(
        _matmul_residual_kernel,
        grid=(B, T // tm, N // tn),
        in_specs=[pl.BlockSpec((None, tm, K), lambda bb, i, j: (bb, i, 0)),
                  pl.BlockSpec((K, tn), lambda bb, i, j: (0, j)),
                  pl.BlockSpec((None, tm, tn), lambda bb, i, j: (bb, i, j)),
                  pl.BlockSpec((None, 1, tn), lambda bb, i, j: (bb, 0, j))],
        out_specs=pl.BlockSpec((None, tm, tn), lambda bb, i, j: (bb, i, j)),
        out_shape=jax.ShapeDtypeStruct((B, T, N), F32),
        compiler_params=pltpu.CompilerParams(
            dimension_semantics=("parallel", "parallel", "parallel"), vmem_limit_bytes=VMEM_LIMIT),
        name="matmul_residual",
    )(a, b, x, gate.reshape(B, 1, N))


def _mod_kernel(c_ref, w_ref, b_ref, o_ref):
    c = c_ref[...]
    c_act = c * jax.nn.sigmoid(c)
    hi = c_act.astype(BF16)
    lo = (c_act - hi.astype(F32)).astype(BF16)
    w = w_ref[...]
    w_hi = w.astype(BF16)
    w_lo = (w - w_hi.astype(F32)).astype(BF16)
    acc = (jnp.dot(lo, w_hi, preferred_element_type=F32) + jnp.dot(hi, w_lo, preferred_element_type=F32)
           + jnp.dot(hi, w_hi, preferred_element_type=F32))
    o_ref[...] = acc + b_ref[...]


def _adaln_mod(c_pad, w_mod, b_mod):
    R, D = c_pad.shape
    N = w_mod.shape[1]
    tn = _tile(N, 512)
    return pl.pallas_call(
        _mod_kernel,
        grid=(N // tn,),
        in_specs=[pl.BlockSpec((R, D), lambda j: (0, 0)),
                  pl.BlockSpec((D, tn), lambda j: (0, j)),
                  pl.BlockSpec((1, tn), lambda j: (0, j))],
        out_specs=pl.BlockSpec((R, tn), lambda j: (0, j)),
        out_shape=jax.ShapeDtypeStruct((R, N), F32),
        compiler_params=pltpu.CompilerParams(
            dimension_semantics=("parallel",), vmem_limit_bytes=VMEM_LIMIT),
        name="adaln_mod",
    )(c_pad, w_mod, b_mod.reshape(1, N))


LOG2E = 1.4426950408889634


def _pair_rms(x, lo_half, w):
    ss = x * x
    s_lo = jnp.sum(jnp.where(lo_half, ss, 0.0), axis=1, keepdims=True)
    s_hi = jnp.sum(jnp.where(lo_half, 0.0, ss), axis=1, keepdims=True)
    ms = jnp.where(lo_half, s_lo, s_hi) * (1.0 / HEAD_DIM)
    return x * lax.rsqrt(ms + NORM_EPS) * w


def _fox_attn_kernel(q_ref, k_ref, v_ref, c_ref, z_ref, qw_ref, kw_ref, o_ref, kn_ref, va_ref, *, tq, tb):
    qi = pl.program_id(2)
    T = k_ref.shape[0]

    @pl.when(qi == 0)
    def _():
        lane_b = lax.broadcasted_iota(jnp.int32, (tb, LANES), 1)
        lo_b = lane_b < HEAD_DIM

        def build(cidx, carry):
            rows = pl.ds(pl.multiple_of(cidx * tb, tb), tb)
            kn_ref[rows, :] = _pair_rms(k_ref[rows, :].astype(F32), lo_b, kw_ref[...]).astype(BF16)
            v2 = v_ref[rows, :].astype(F32)
            va_ref[0, rows, :] = jnp.where(lo_b, v2, jnp.where(lane_b == HEAD_DIM, 1.0, 0.0)).astype(BF16)
            va_ref[1, rows, :] = jnp.where(lo_b, jnp.where(lane_b == 0, 1.0, 0.0), v2).astype(BF16)
            return carry

        lax.fori_loop(0, T // tb, build, 0)

    lane = lax.broadcasted_iota(jnp.int32, (tq, LANES), 1)
    lo = lane < HEAD_DIM
    qn = _pair_rms(q_ref[...].astype(F32), lo, qw_ref[...])
    qh = (jnp.where(lo, qn, 0.0).astype(BF16), jnp.where(lo, 0.0, qn).astype(BF16))
    q_start = pl.multiple_of(qi * tq, tq)
    c_ref0 = [c_ref[h, :, pl.ds(q_start, LANES)][:, 0:1] for h in range(2)]
    row = lax.broadcasted_iota(jnp.int32, (tq, tq), 0)
    col = lax.broadcasted_iota(jnp.int32, (tq, tq), 1)
    causal = col <= row
    nt = (((1,), (1,)), ((), ()))

    def step(j, carry, masked):
        start = pl.multiple_of(j * tq, tq)
        kj = kn_ref[pl.ds(start, tq), :]
        out = []
        for h in range(2):
            m, acc = carry[h]
            bias = (c_ref0[h] - c_ref[h, :, pl.ds(start, tq)]) * LOG2E
            s = lax.dot_general(qh[h], kj, nt, preferred_element_type=F32) + bias
            if masked:
                s = jnp.where(causal, s, NEG_BIG)
            m_new = jnp.maximum(m, jnp.max(s, axis=1, keepdims=True))
            p = jnp.exp2(s - m_new).astype(BF16)
            acc = jnp.exp2(m - m_new) * acc + jnp.dot(p, va_ref[h, pl.ds(start, tq), :],
                                                     preferred_element_type=F32)
            out.append((m_new, acc))
        return tuple(out)

    init = tuple((jnp.full((tq, 1), NEG_BIG, F32), jnp.zeros((tq, LANES), F32)) for _ in range(2))
    carry = lax.fori_loop(0, qi, functools.partial(step, masked=False), init)
    (_, acc0), (_, acc1) = step(qi, carry, True)
    o = jnp.where(lo, acc0 / acc0[:, HEAD_DIM:HEAD_DIM + 1], acc1 / acc1[:, 0:1])
    z = z_ref[...].astype(F32)
    o_ref[...] = (o * (z * jax.nn.sigmoid(z))).astype(o_ref.dtype)


def _fox_attention(qk, v, cum_rows, z, q_norm_w, k_norm_w):
    B, T, D = v.shape
    H = D // HEAD_DIM
    tq = _tile(T, 512)
    n_pair = D // LANES
    qw = jnp.tile(q_norm_w, 2).reshape(1, LANES) * (HEAD_DIM ** -0.5 * LOG2E)
    kw = jnp.tile(k_norm_w, 2).reshape(1, LANES)
    return pl.pallas_call(
        functools.partial(_fox_attn_kernel, tq=tq, tb=tq),
        grid=(B, H // 2, T // tq),
        in_specs=[pl.BlockSpec((None, tq, LANES), lambda b, hp, i: (b, i, hp)),
                  pl.BlockSpec((None, T, LANES), lambda b, hp, i: (b, 0, n_pair + hp)),
                  pl.BlockSpec((None, T, LANES), lambda b, hp, i: (b, 0, hp)),
                  pl.BlockSpec((None, 2, 1, T), lambda b, hp, i: (b, hp, 0, 0)),
                  pl.BlockSpec((None, tq, LANES), lambda b, hp, i: (b, i, hp)),
                  pl.BlockSpec((1, LANES), lambda b, hp, i: (0, 0)),
                  pl.BlockSpec((1, LANES), lambda b, hp, i: (0, 0))],
        out_specs=pl.BlockSpec((None, tq, LANES), lambda b, hp, i: (b, i, hp)),
        out_shape=jax.ShapeDtypeStruct((B, T, D), BF16),
        scratch_shapes=[pltpu.VMEM((T, LANES), BF16), pltpu.VMEM((2, T, LANES), BF16)],
        compiler_params=pltpu.CompilerParams(
            dimension_semantics=("parallel", "parallel", "arbitrary"), vmem_limit_bytes=VMEM_LIMIT),
        name="fox_attention",
    )(qk, qk, v, cum_rows, z, qw, kw)


def _trunc_bf16(x):
    bits = lax.bitcast_convert_type(x, jnp.uint32) & jnp.uint32(0xFFFF0000)
    return lax.bitcast_convert_type(bits, F32)


def _split3(x):
    hi = _trunc_bf16(x)
    r1 = x - hi
    mid = _trunc_bf16(r1)
    lo = r1 - mid
    return hi.astype(BF16), mid.astype(BF16), lo.astype(BF16)


_NN = (((1,), (0,)), ((), ()))
_NT = (((1,), (1,)), ((), ()))
_TN = (((0,), (0,)), ((), ()))


def _split2(x):
    hi = x.astype(BF16)
    return hi, (x - hi.astype(F32)).astype(BF16)


def _dot3(a, b, dims=_NN):
    a_hi, a_lo = _split2(a)
    b_hi, b_lo = _split2(b)
    dg = functools.partial(lax.dot_general, dimension_numbers=dims, preferred_element_type=F32)
    return (dg(a_lo, b_hi) + dg(a_hi, b_lo)) + dg(a_hi, b_hi)


def _rwkv_chunk(r, lw, k, v, a, b, S):
    C, W = r.shape
    t_idx = lax.broadcasted_iota(jnp.int32, (C, W), 0)
    lane = lax.broadcasted_iota(jnp.int32, (C, W), 1)
    i_idx = lane & (HEAD_DIM - 1)
    lane_head = lane >> 6
    strict = i_idx < t_idx
    incl = i_idx <= t_idx
    n_heads = W // HEAD_DIM

    def bdrows(x):
        return jnp.concatenate([jnp.where(lane_head == h, x, 0.0) for h in range(n_heads)], axis=0)

    rr = lax.broadcasted_iota(jnp.int32, (C, C), 0)
    cc = lax.broadcasted_iota(jnp.int32, (C, C), 1)
    tri = jnp.where(cc <= rr, 1.0, 0.0).astype(BF16)
    lw_hi, lw_mid, lw_lo = _split3(lw)
    mm = functools.partial(jnp.dot, preferred_element_type=F32)
    cum = (mm(tri, lw_lo) + mm(tri, lw_mid)) + mm(tri, lw_hi)
    cum_end = cum[C - 1:C, :]
    e_pos = jnp.exp(cum)
    e_neg = jnp.exp(-cum)
    e_end = jnp.exp(cum_end - cum)
    At = a * jnp.exp(cum - lw)
    Rt = r * e_pos
    Bt = b * e_neg
    Kt = k * e_neg
    Bh = b * e_end
    Kh = k * e_end
    wc = jnp.exp(cum_end)

    AR = jnp.concatenate([At, Rt], axis=0)
    sB = _dot3(AR, bdrows(Bt), _NT)
    sK = _dot3(AR, bdrows(Kt), _NT)
    A_ab = jnp.where(strict, sB[:C], 0.0)
    A_rb = jnp.where(incl, sB[C:], 0.0)
    A_ak = jnp.where(strict, sK[:C], 0.0)
    A_rk = jnp.where(incl, sK[C:], 0.0)

    P = jnp.where(i_idx == t_idx, 1.0, 0.0) + A_ab
    Apow = A_ab
    n_sq = max(C.bit_length() - 2, 0)
    for _ in range(n_sq):
        Apow = _dot3(Apow, bdrows(Apow))
        P = P + _dot3(Apow, bdrows(P))

    vbd = bdrows(v)
    X = _dot3(At, S, _NT) + _dot3(A_ak, vbd)
    U = _dot3(P, bdrows(X))
    Y = _dot3(Rt, S, _NT) + _dot3(A_rb, bdrows(U)) + _dot3(A_rk, vbd)
    rw = lax.broadcasted_iota(jnp.int32, (W, W), 0) >> 6
    cw = lax.broadcasted_iota(jnp.int32, (W, W), 1) >> 6
    upd = _dot3(jnp.concatenate([U, v], axis=0), jnp.concatenate([Bh, Kh], axis=0), _TN)
    S_new = S * wc + jnp.where(rw == cw, upd, 0.0)
    return Y, S_new


def _rwkv_kernel(r_ref, lw_ref, k_ref, v_ref, a_ref, b_ref, y_ref, s_ref):
    @pl.when(pl.program_id(2) == 0)
    def _():
        s_ref[...] = jnp.zeros_like(s_ref)

    y, s_new = _rwkv_chunk(r_ref[...], lw_ref[...], k_ref[...], v_ref[...], a_ref[...], b_ref[...],
                           s_ref[...])
    y_ref[...] = y
    s_ref[...] = s_new


def _rwkv_recurrence(r, lw, k, v, a, b):
    B, T, D = r.shape
    C, W = RWKV_CHUNK, MXU_DIM
    spec = pl.BlockSpec((None, C, W), lambda bb, g, c: (bb, c, g))
    return pl.pallas_call(
        _rwkv_kernel,
        grid=(B, D // W, T // C),
        in_specs=[spec] * 6,
        out_specs=spec,
        out_shape=jax.ShapeDtypeStruct((B, T, D), F32),
        scratch_shapes=[pltpu.VMEM((W, W), F32)],
        compiler_params=pltpu.CompilerParams(
            dimension_semantics=("parallel", "parallel", "arbitrary"), vmem_limit_bytes=VMEM_LIMIT),
        name="rwkv7_recurrence",
    )(r, lw, k, v, a, b)


def _fox_layer(x, h, gate, w_in, b_f, q_norm_w, k_norm_w, w_out):
    B, T, D = x.shape
    H = D // HEAD_DIM
    M = B * T
    hb = h.reshape(M, D)
    w_b = w_in.astype(BF16)
    qk = _matmul(hb, w_b[:, :2 * D], BF16).reshape(B, T, 2 * D)
    v = _matmul(hb, w_b[:, 2 * D:3 * D], F32).reshape(B, T, D)
    z = _matmul(hb, w_b[:, 3 * D:4 * D], BF16).reshape(B, T, D)
    f_logit = _matmul(hb, _pad_cols(w_b[:, 4 * D:], LANES), F32)[:, :H].reshape(B, T, H) + b_f
    cum = jnp.cumsum(jax.nn.log_sigmoid(f_logit), axis=1)
    cum_rows = jnp.transpose(cum, (0, 2, 1)).reshape(B, H, 1, T)
    g = _fox_attention(qk, v, cum_rows, z, q_norm_w, k_norm_w)
    return _matmul_residual(g, w_out.astype(BF16), x, gate), v


def _pad_cols(w, n):
    return jnp.pad(w, ((0, 0), (0, n - w.shape[1])))


def _pad_rows(w, n):
    return jnp.pad(w, ((0, n - w.shape[0]), (0, 0)))


def _lora(xb, w1, w2, act):
    rank = w1.shape[1]
    rp = -(-rank // LANES) * LANES
    mid = act(_matmul(xb, _pad_cols(w1, rp).astype(BF16)))
    return _matmul(mid.astype(BF16), _pad_rows(w2, rp).astype(BF16))


def _rwkv_layer(x, h, gate, v_first, mu, w_in, w0, w1, w2, a0, a1, a2, v0, v1, v2,
                k_k, k_a, r_k, lnx_w, lnx_b, w_out):
    B, T, D = x.shape
    H, N = D // HEAD_DIM, HEAD_DIM
    M = B * T
    xx = jnp.pad(h[:, :-1], ((0, 0), (1, 0), (0, 0))) - h
    xr, xw, xk, xv, xa, xg = [(h + xx * mu[i]).reshape(M, D).astype(BF16) for i in range(6)]
    r = _matmul(xr, w_in[0].astype(BF16))
    k = _matmul(xk, w_in[1].astype(BF16))
    v = _matmul(xv, w_in[2].astype(BF16))
    z = _matmul(xg, w_in[3].astype(BF16))
    w_log = -jax.nn.softplus(-(w0 + _lora(xw, w1, w2, jnp.tanh))) - 0.5
    lw = -jnp.exp(w_log)
    v = v + (v_first.reshape(M, D) - v) * jax.nn.sigmoid(v0 + _lora(xv, v1, v2, lambda t: t))
    a = jax.nn.sigmoid(a0 + _lora(xa, a1, a2, lambda t: t))
    kk = (k * k_k).reshape(M, H, N)
    kk = (kk / jnp.maximum(jnp.sqrt(jnp.sum(kk * kk, axis=-1, keepdims=True)), 1e-12)).reshape(M, D)
    k = k * (1.0 + (a - 1.0) * k_a)
    shp = (B, T, D)
    y = _rwkv_recurrence(r.reshape(shp), lw.reshape(shp), k.reshape(shp), v.reshape(shp),
                         (-kk).reshape(shp), (kk * a).reshape(shp))
    y = y.reshape(M, H, N)
    mean = jnp.mean(y, axis=-1, keepdims=True)
    var = jnp.mean(jnp.square(y - mean), axis=-1, keepdims=True)
    y = (y - mean) * lax.rsqrt(var + LNX_EPS)
    y = y * lnx_w.reshape(H, N) + lnx_b.reshape(H, N)
    rh, kh, vh = r.reshape(M, H, N), k.reshape(M, H, N), v.reshape(M, H, N)
    y = y + jnp.sum(rh * kh * r_k, axis=-1, keepdims=True) * vh
    g = (y.reshape(M, D) * jax.nn.silu(z)).astype(BF16).reshape(B, T, D)
    return _matmul_residual(g, w_out.astype(BF16), x, gate)


def kernel(x, c, norm_w, w_mod, b_mod, fox_w_in, fox_b_f, fox_q_norm_w, fox_k_norm_w, fox_w_out, rwkv_mu, rwkv_w_in, rwkv_w0, rwkv_w1, rwkv_w2, rwkv_a0, rwkv_a1, rwkv_a2, rwkv_v0, rwkv_v1, rwkv_v2, rwkv_k_k, rwkv_k_a, rwkv_r_k, rwkv_lnx_w, rwkv_lnx_b, rwkv_w_out):
    B, T, D = x.shape
    depth = norm_w.shape[0]
    c_pad = jnp.pad(c, ((0, 8 - B % 8 if B % 8 else 0), (0, 0)))
    v_first = None
    for i in range(depth):
        mod = _adaln_mod(c_pad, w_mod[i], b_mod[i])[:B]
        shift, scale, gate = mod[:, :D], mod[:, D:2 * D], mod[:, 2 * D:]
        h = _norm_mod(x, norm_w[i], scale, shift, BF16 if i % 2 == 0 else F32)
        j = i // 2
        if i % 2 == 0:
            x, v = _fox_layer(x, h, gate, fox_w_in[j], fox_b_f[j], fox_q_norm_w[j], fox_k_norm_w[j],
                              fox_w_out[j])
            if v_first is None:
                v_first = v
        else:
            x = _rwkv_layer(x, h, gate, v_first, rwkv_mu[j], rwkv_w_in[j], rwkv_w0[j], rwkv_w1[j],
                            rwkv_w2[j], rwkv_a0[j], rwkv_a1[j], rwkv_a2[j], rwkv_v0[j], rwkv_v1[j],
                            rwkv_v2[j], rwkv_k_k[j], rwkv_k_a[j], rwkv_r_k[j], rwkv_lnx_w[j],
                            rwkv_lnx_b[j], rwkv_w_out[j])
    return x
```

```python
import functools

import jax
import jax.numpy as jnp
from jax import lax
from jax.experimental import pallas as pl
from jax.experimental.pallas import tpu as pltpu

F32 = jnp.float32
BF16 = jnp.bfloat16

HEAD_DIM = 64
LANES = 128
MXU_DIM = 256
RWKV_CHUNK = 64
HEADS_PER_GROUP = MXU_DIM // HEAD_DIM
NORM_EPS = 1e-6
LNX_EPS = 64e-5
VMEM_LIMIT = 56 * 1024 * 1024
NEG_BIG = -1e30


def _tile(n, pref):
    t = min(n, pref)
    while n % t:
        t //= 2
    return t


def _norm_mod_kernel(x_ref, nw_ref, sc_ref, sh_ref, o_ref):
    x = x_ref[...]
    ms = jnp.mean(x * x, axis=-1, keepdims=True)
    y = x * lax.rsqrt(ms + NORM_EPS) * nw_ref[...]
    o_ref[...] = (y * (1.0 + sc_ref[...]) + sh_ref[...]).astype(o_ref.dtype)


def _norm_mod(x, norm_w, scale, shift, out_dtype):
    B, T, D = x.shape
    tm = _tile(T, 512)
    row = pl.BlockSpec((None, tm, D), lambda b, i: (b, i, 0))
    per_b = pl.BlockSpec((None, 1, D), lambda b, i: (b, 0, 0))
    return pl.pallas_call(
        _norm_mod_kernel,
        grid=(B, T // tm),
        in_specs=[row, pl.BlockSpec((1, D), lambda b, i: (0, 0)), per_b, per_b],
        out_specs=row,
        out_shape=jax.ShapeDtypeStruct((B, T, D), out_dtype),
        compiler_params=pltpu.CompilerParams(
            dimension_semantics=("parallel", "parallel"), vmem_limit_bytes=VMEM_LIMIT),
        name="norm_mod",
    )(x, norm_w.reshape(1, D), scale.reshape(B, 1, D), shift.reshape(B, 1, D))


def _matmul_kernel(a_ref, b_ref, o_ref):
    o_ref[...] = jnp.dot(a_ref[...], b_ref[...], preferred_element_type=F32).astype(o_ref.dtype)


def _matmul(a, b, out_dtype=F32):
    M, K = a.shape
    N = b.shape[1]
    tm, tn = _tile(M, 1024), _tile(N, 512)
    return pl.pallas_call(
        _matmul_kernel,
        grid=(M // tm, N // tn),
        in_specs=[pl.BlockSpec((tm, K), lambda i, j: (i, 0)),
                  pl.BlockSpec((K, tn), lambda i, j: (0, j))],
        out_specs=pl.BlockSpec((tm, tn), lambda i, j: (i, j)),
        out_shape=jax.ShapeDtypeStruct((M, N), out_dtype),
        compiler_params=pltpu.CompilerParams(
            dimension_semantics=("parallel", "parallel"), vmem_limit_bytes=VMEM_LIMIT),
        name="matmul",
    )(a, b)


def _matmul_residual_kernel(a_ref, b_ref, x_ref, g_ref, o_ref):
    acc = jnp.dot(a_ref[...], b_ref[...], preferred_element_type=F32)
    o_ref[...] = x_ref[...] + g_ref[...] * acc


def _matmul_residual(a, b, x, gate):
    B, T, K = a.shape
    N = b.shape[1]
    tm, tn = _tile(T, 1024), _tile(N, 512)
    return pl.pallas_call(
        _matmul_residual_kernel,
        grid=(B, T // tm, N // tn),
        in_specs=[pl.BlockSpec((None, tm, K), lambda bb, i, j: (bb, i, 0)),
                  pl.BlockSpec((K, tn), lambda bb, i, j: (0, j)),
                  pl.BlockSpec((None, tm, tn), lambda bb, i, j: (bb, i, j)),
                  pl.BlockSpec((None, 1, tn), lambda bb, i, j: (bb, 0, j))],
        out_specs=pl.BlockSpec((None, tm, tn), lambda bb, i, j: (bb, i, j)),
        out_shape=jax.ShapeDtypeStruct((B, T, N), F32),
        compiler_params=pltpu.CompilerParams(
            dimension_semantics=("parallel", "parallel", "parallel"), vmem_limit_bytes=VMEM_LIMIT),
        name="matmul_residual",
    )(a, b, x, gate.reshape(B, 1, N))


def _mod_kernel(c_ref, w_ref, b_ref, o_ref):
    c = c_ref[...]
    c_act = c * jax.nn.sigmoid(c)
    hi = c_act.astype(BF16)
    lo = (c_act - hi.astype(F32)).astype(BF16)
    w = w_ref[...]
    w_hi = w.astype(BF16)
    w_lo = (w - w_hi.astype(F32)).astype(BF16)
    acc = (jnp.dot(lo, w_hi, preferred_element_type=F32) + jnp.dot(hi, w_lo, preferred_element_type=F32)
           + jnp.dot(hi, w_hi, preferred_element_type=F32))
    o_ref[...] = acc + b_ref[...]


def _adaln_mod(c_pad, w_mod, b_mod):
    R, D = c_pad.shape
    N = w_mod.shape[1]
    tn = _tile(N, 512)
    return pl.pallas_call(
        _mod_kernel,
        grid=(N // tn,),
        in_specs=[pl.BlockSpec((R, D), lambda j: (0, 0)),
                  pl.BlockSpec((D, tn), lambda j: (0, j)),
                  pl.BlockSpec((1, tn), lambda j: (0, j))],
        out_specs=pl.BlockSpec((R, tn), lambda j: (0, j)),
        out_shape=jax.ShapeDtypeStruct((R, N), F32),
        compiler_params=pltpu.CompilerParams(
            dimension_semantics=("parallel",), vmem_limit_bytes=VMEM_LIMIT),
        name="adaln_mod",
    )(c_pad, w_mod, b_mod.reshape(1, N))


LOG2E = 1.4426950408889634


def _pair_rms(x, lo_half, w):
    ss = x * x
    s_lo = jnp.sum(jnp.where(lo_half, ss, 0.0), axis=1, keepdims=True)
    s_hi = jnp.sum(jnp.where(lo_half, 0.0, ss), axis=1, keepdims=True)
    ms = jnp.where(lo_half, s_lo, s_hi) * (1.0 / HEAD_DIM)
    return x * lax.rsqrt(ms + NORM_EPS) * w


def _fox_attn_kernel(q_ref, k_ref, v_ref, c_ref, z_ref, qw_ref, kw_ref, o_ref, kn_ref, va_ref, m_ref, acc_ref,
                     *, tq, tb):
    qi = pl.program_id(2)
    T = k_ref.shape[0]

    @pl.when(qi == 0)
    def _():
        lane_b = lax.broadcasted_iota(jnp.int32, (tb, LANES), 1)
        lo_b = lane_b < HEAD_DIM

        def build(cidx, carry):
            rows = pl.ds(pl.multiple_of(cidx * tb, tb), tb)
            kn_ref[rows, :] = _pair_rms(k_ref[rows, :].astype(F32), lo_b, kw_ref[...]).astype(BF16)
            v2 = v_ref[rows, :].astype(F32)
            va_ref[0, rows, :] = jnp.where(lo_b, v2, jnp.where(lane_b == HEAD_DIM, 1.0, 0.0)).astype(BF16)
            va_ref[1, rows, :] = jnp.where(lo_b, jnp.where(lane_b == 0, 1.0, 0.0), v2).astype(BF16)
            return carry

        lax.fori_loop(0, T // tb, build, 0)

    lane = lax.broadcasted_iota(jnp.int32, (tq, LANES), 1)
    lo = lane < HEAD_DIM
    qn = _pair_rms(q_ref[...].astype(F32), lo, qw_ref[...])
    qh = (jnp.where(lo, qn, 0.0).astype(BF16), jnp.where(lo, 0.0, qn).astype(BF16))
    q_start = pl.multiple_of(qi * tq, tq)
    c_ref0 = [c_ref[h, :, pl.ds(q_start, LANES)][:, 0:1] for h in range(2)]
    row = lax.broadcasted_iota(jnp.int32, (tq, tq), 0)
    col = lax.broadcasted_iota(jnp.int32, (tq, tq), 1)
    causal = col <= row
    nt = (((1,), (1,)), ((), ()))

    m_ref[...] = jnp.full(m_ref.shape, NEG_BIG, F32)
    acc_ref[...] = jnp.zeros(acc_ref.shape, F32)

    def step(start, tk, masked):
        kj = kn_ref[pl.ds(start, tk), :]
        for h in range(2):
            bias = (c_ref0[h] - c_ref[h, :, pl.ds(start, tk)]) * LOG2E
            s = lax.dot_general(qh[h], kj, nt, preferred_element_type=F32) + bias
            if masked:
                s = jnp.where(causal, s, NEG_BIG)
            m = m_ref[h]
            m_new = jnp.maximum(m, jnp.max(s, axis=1, keepdims=True))
            p = jnp.exp2((s - m_new).astype(BF16))
            acc_ref[h] = jnp.exp2(m - m_new) * acc_ref[h] + jnp.dot(
                p, va_ref[h, pl.ds(start, tk), :], preferred_element_type=F32)
            m_ref[h] = m_new

    def body(j, carry):
        step(pl.multiple_of(j * tq, tq), tq, False)
        return carry

    lax.fori_loop(0, qi, body, 0)
    step(q_start, tq, True)
    acc0, acc1 = acc_ref[0], acc_ref[1]
    o = jnp.where(lo, acc0 / acc0[:, HEAD_DIM:HEAD_DIM + 1], acc1 / acc1[:, 0:1])
    z = z_ref[...].astype(F32)
    o_ref[...] = (o * (z * jax.nn.sigmoid(z))).astype(o_ref.dtype)


def _fox_attention(qk, v, cum_rows, z, q_norm_w, k_norm_w):
    B, T, D = v.shape
    H = D // HEAD_DIM
    tq = _tile(T, 1024)
    n_pair = D // LANES
    qw = jnp.tile(q_norm_w, 2).reshape(1, LANES) * (HEAD_DIM ** -0.5 * LOG2E)
    kw = jnp.tile(k_norm_w, 2).reshape(1, LANES)
    once = pl.Buffered(1)
    return pl.pallas_call(
        functools.partial(_fox_attn_kernel, tq=tq, tb=_tile(T, 512)),
        grid=(B, H // 2, T // tq),
        in_specs=[pl.BlockSpec((None, tq, LANES), lambda b, hp, i: (b, i, hp)),
                  pl.BlockSpec((None, T, LANES), lambda b, hp, i: (b, 0, n_pair + hp), pipeline_mode=once),
                  pl.BlockSpec((None, T, LANES), lambda b, hp, i: (b, 0, hp), pipeline_mode=once),
                  pl.BlockSpec((None, 2, 1, T), lambda b, hp, i: (b, hp, 0, 0), pipeline_mode=once),
                  pl.BlockSpec((None, tq, LANES), lambda b, hp, i: (b, i, hp)),
                  pl.BlockSpec((1, LANES), lambda b, hp, i: (0, 0)),
                  pl.BlockSpec((1, LANES), lambda b, hp, i: (0, 0))],
        out_specs=pl.BlockSpec((None, tq, LANES), lambda b, hp, i: (b, i, hp)),
        out_shape=jax.ShapeDtypeStruct((B, T, D), BF16),
        scratch_shapes=[pltpu.VMEM((T, LANES), BF16), pltpu.VMEM((2, T, LANES), BF16),
                        pltpu.VMEM((2, tq, 1), F32), pltpu.VMEM((2, tq, LANES), F32)],
        compiler_params=pltpu.CompilerParams(
            dimension_semantics=("parallel", "parallel", "arbitrary"), vmem_limit_bytes=VMEM_LIMIT),
        name="fox_attention",
    )(qk, qk, v, cum_rows, z, qw, kw)


def _trunc_bf16(x):
    bits = lax.bitcast_convert_type(x, jnp.uint32) & jnp.uint32(0xFFFF0000)
    return lax.bitcast_convert_type(bits, F32)


def _split3(x):
    hi = _trunc_bf16(x)
    r1 = x - hi
    mid = _trunc_bf16(r1)
    lo = r1 - mid
    return hi.astype(BF16), mid.astype(BF16), lo.astype(BF16)


_NN = (((1,), (0,)), ((), ()))
_NT = (((1,), (1,)), ((), ()))
_TN = (((0,), (0,)), ((), ()))


def _hl(x):
    hi = x.astype(BF16)
    return hi, (x - hi.astype(F32)).astype(BF16)


def _mm3(a, b, dims=_NN):
    dg = functools.partial(lax.dot_general, dimension_numbers=dims, preferred_element_type=F32)
    return (dg(a[1], b[0]) + dg(a[0], b[1])) + dg(a[0], b[0])


def _rwkv_chunks(groups, head_masks):
    C, W = groups[0][0].shape
    n = range(len(groups))
    t_idx = lax.broadcasted_iota(jnp.int32, (C, W), 0)
    i_idx = lax.broadcasted_iota(jnp.int32, (C, W), 1) & (HEAD_DIM - 1)
    strict = i_idx < t_idx
    incl = i_idx <= t_idx
    eye = jnp.where(i_idx == t_idx, 1.0, 0.0)
    rw = lax.broadcasted_iota(jnp.int32, (W, W), 0) >> 6
    cw = lax.broadcasted_iota(jnp.int32, (W, W), 1) >> 6
    same_head = rw == cw

    def bd(parts):
        return tuple(jnp.concatenate([p * mk for mk in head_masks], axis=0) for p in parts)

    rr = lax.broadcasted_iota(jnp.int32, (C, C), 0)
    cc = lax.broadcasted_iota(jnp.int32, (C, C), 1)
    tri = jnp.where(cc <= rr, 1.0, 0.0).astype(BF16)
    mm = functools.partial(jnp.dot, preferred_element_type=F32)

    r, lw, k, v, a, b, S = [[grp[i] for grp in groups] for i in range(7)]
    lw3 = [_split3(lw[g]) for g in n]
    cum = [(mm(tri, lw3[g][2]) + mm(tri, lw3[g][1])) + mm(tri, lw3[g][0]) for g in n]
    e_neg = [jnp.exp(-cum[g]) for g in n]
    AR = [_hl(jnp.concatenate([a[g] * jnp.exp(cum[g] - lw[g]), r[g] * jnp.exp(cum[g])], axis=0))
          for g in n]
    sB = [_mm3(AR[g], bd(_hl(b[g] * e_neg[g])), _NT) for g in n]
    sK = [_mm3(AR[g], bd(_hl(k[g] * e_neg[g])), _NT) for g in n]
    A_ab = [jnp.where(strict, sB[g][:C], 0.0) for g in n]

    P = [eye + A_ab[g] for g in n]
    Apow = [_hl(A_ab[g]) for g in n]
    for _ in range(max(C.bit_length() - 2, 0)):
        Apow = [_hl(_mm3(Apow[g], bd(Apow[g]))) for g in n]
        P = [P[g] + _mm3(Apow[g], bd(_hl(P[g]))) for g in n]

    A_k = [_hl(jnp.concatenate([jnp.where(strict, sK[g][:C], 0.0), jnp.where(incl, sK[g][C:], 0.0)], axis=0))
           for g in n]
    XY = [_mm3(AR[g], _hl(S[g]), _NT) + _mm3(A_k[g], bd(_hl(v[g]))) for g in n]
    U = [_mm3(_hl(P[g]), bd(_hl(XY[g][:C]))) for g in n]
    Y = [XY[g][C:] + _mm3(_hl(jnp.where(incl, sB[g][C:], 0.0)), bd(_hl(U[g]))) for g in n]
    out = []
    for g in n:
        cum_end = cum[g][C - 1:C, :]
        e_end = jnp.exp(cum_end - cum[g])
        upd = _mm3(_hl(jnp.concatenate([U[g], v[g]], axis=0)),
                   _hl(jnp.concatenate([b[g] * e_end, k[g] * e_end], axis=0)), _TN)
        out.append((Y[g], S[g] * jnp.exp(cum_end) + jnp.where(same_head, upd, 0.0)))
    return out


_RWKV_ROW_INPUTS = 8
_RWKV_VEC_PARAMS = 8


def _rwkv_kernel(*refs):
    row_refs = refs[:_RWKV_ROW_INPUTS]
    vec_refs = refs[_RWKV_ROW_INPUTS:_RWKV_ROW_INPUTS + _RWKV_VEC_PARAMS]
    o_ref, s_ref = refs[_RWKV_ROW_INPUTS + _RWKV_VEC_PARAMS:]

    @pl.when(pl.program_id(2) == 0)
    def _():
        s_ref[...] = jnp.zeros_like(s_ref)

    C, W = row_refs[0].shape[0], MXU_DIM
    lane_head = lax.broadcasted_iota(jnp.int32, (C, W), 1) >> 6
    in_head = [lane_head == h for h in range(HEADS_PER_GROUP)]
    head_masks = [jnp.where(m, 1.0, 0.0).astype(BF16) for m in in_head]
    col_slices = [slice(g * W, (g + 1) * W) for g in range(row_refs[0].shape[1] // W)]
    n = range(len(col_slices))

    def head_sum(x):
        parts = [jnp.sum(jnp.where(m, x, 0.0), axis=1, keepdims=True) for m in in_head]
        out = parts[-1]
        for m, p in zip(in_head[-2::-1], parts[-2::-1]):
            out = jnp.where(m, p, out)
        return out

    rows = [[ref[:, cols].astype(F32) for ref in row_refs] for cols in col_slices]
    vecs = [[ref[:, cols] for ref in vec_refs] for cols in col_slices]
    groups, keep = [], []
    for g in n:
        r, k, v, wl, al, vl, vf, z = rows[g]
        w0, a0, v0, k_k, k_a, r_k, _, _ = vecs[g]
        t = -(w0 + wl)
        softplus = jnp.maximum(t, 0.0) + jnp.log(1.0 + jnp.exp(-jnp.abs(t)))
        lw = -jnp.exp(-softplus - 0.5)
        rate = jax.nn.sigmoid(a0 + al)
        vv = v + (vf - v) * jax.nn.sigmoid(v0 + vl)
        kk = k * k_k
        kk = kk / jnp.maximum(jnp.sqrt(head_sum(kk * kk)), 1e-12)
        k2 = k * (1.0 + (rate - 1.0) * k_a)
        groups.append((r, lw, k2, vv, -kk, kk * rate, s_ref[g]))
        keep.append((head_sum(r * k2 * r_k) * vv, z))
    results = _rwkv_chunks(groups, head_masks)
    for g, cols in enumerate(col_slices):
        y, s_new = results[g]
        bonus, z = keep[g]
        lnx_w, lnx_b = vecs[g][6], vecs[g][7]
        d = y - head_sum(y) * (1.0 / HEAD_DIM)
        var = head_sum(d * d) * (1.0 / HEAD_DIM)
        yn = d * lax.rsqrt(var + LNX_EPS) * lnx_w + lnx_b
        o_ref[:, cols] = ((yn + bonus) * (z * jax.nn.sigmoid(z))).astype(o_ref.dtype)
        s_ref[g] = s_new


def _rwkv_mix(r, k, v, wl, al, vl, v_first, z, vec_params):
    B, T, D = r.shape
    C, W = RWKV_CHUNK, MXU_DIM
    groups = next(g for g in (4, 2, 1) if D % (g * W) == 0)
    spec = pl.BlockSpec((None, C, groups * W), lambda bb, g, c: (bb, c, g))
    vec_spec = pl.BlockSpec((1, groups * W), lambda bb, g, c: (0, g))
    return pl.pallas_call(
        _rwkv_kernel,
        grid=(B, D // (groups * W), T // C),
        in_specs=[spec] * _RWKV_ROW_INPUTS + [vec_spec] * _RWKV_VEC_PARAMS,
        out_specs=spec,
        out_shape=jax.ShapeDtypeStruct((B, T, D), BF16),
        scratch_shapes=[pltpu.VMEM((groups, W, W), F32)],
        compiler_params=pltpu.CompilerParams(
            dimension_semantics=("parallel", "parallel", "arbitrary"), vmem_limit_bytes=VMEM_LIMIT),
        name="rwkv7_mix",
    )(r, k, v, wl, al, vl, v_first, z, *[p.reshape(1, D) for p in vec_params])


def _fox_layer(x, h, gate, w_in, b_f, q_norm_w, k_norm_w, w_out):
    B, T, D = x.shape
    H = D // HEAD_DIM
    M = B * T
    hb = h.reshape(M, D)
    w_b = w_in.astype(BF16)
    qk = _matmul(hb, w_b[:, :2 * D], BF16).reshape(B, T, 2 * D)
    v = _matmul(hb, w_b[:, 2 * D:3 * D], F32).reshape(B, T, D)
    z = _matmul(hb, w_b[:, 3 * D:4 * D], BF16).reshape(B, T, D)
    f_logit = _matmul(hb, _pad_cols(w_b[:, 4 * D:], LANES), F32)[:, :H].reshape(B, T, H) + b_f
    cum = jnp.cumsum(jax.nn.log_sigmoid(f_logit), axis=1)
    cum_rows = jnp.transpose(cum, (0, 2, 1)).reshape(B, H, 1, T)
    g = _fox_attention(qk, v.astype(BF16), cum_rows, z, q_norm_w, k_norm_w)
    return _matmul_residual(g, w_out.astype(BF16), x, gate), v


def _pad_cols(w, n):
    return jnp.pad(w, ((0, 0), (0, n - w.shape[1])))


def _pad_rows(w, n):
    return jnp.pad(w, ((0, n - w.shape[0]), (0, 0)))


def _lora(xb, w1, w2, act):
    rank = w1.shape[1]
    rp = -(-rank // LANES) * LANES
    mid = act(_matmul(xb, _pad_cols(w1, rp).astype(BF16)))
    return _matmul(mid.astype(BF16), _pad_rows(w2, rp).astype(BF16))


def _rwkv_layer(x, h, gate, v_first, mu, w_in, w0, w1, w2, a0, a1, a2, v0, v1, v2,
                k_k, k_a, r_k, lnx_w, lnx_b, w_out):
    B, T, D = x.shape
    M = B * T
    shp = (B, T, D)
    xx = jnp.pad(h[:, :-1], ((0, 0), (1, 0), (0, 0))) - h
    xr, xw, xk, xv, xa, xg = [(h + xx * mu[i]).reshape(M, D).astype(BF16) for i in range(6)]
    r = _matmul(xr, w_in[0].astype(BF16)).reshape(shp)
    k = _matmul(xk, w_in[1].astype(BF16)).reshape(shp)
    v = _matmul(xv, w_in[2].astype(BF16)).reshape(shp)
    z = _matmul(xg, w_in[3].astype(BF16), BF16).reshape(shp)
    wl = _lora(xw, w1, w2, jnp.tanh).reshape(shp)
    al = _lora(xa, a1, a2, lambda t: t).reshape(shp)
    vl = _lora(xv, v1, v2, lambda t: t).reshape(shp)
    g = _rwkv_mix(r, k, v, wl, al, vl, v_first, z,
                  (w0, a0, v0, k_k, k_a, r_k.reshape(D), lnx_w, lnx_b))
    return _matmul_residual(g, w_out.astype(BF16), x, gate)


def kernel(x, c, norm_w, w_mod, b_mod, fox_w_in, fox_b_f, fox_q_norm_w, fox_k_norm_w, fox_w_out, rwkv_mu, rwkv_w_in, rwkv_w0, rwkv_w1, rwkv_w2, rwkv_a0, rwkv_a1, rwkv_a2, rwkv_v0, rwkv_v1, rwkv_v2, rwkv_k_k, rwkv_k_a, rwkv_r_k, rwkv_lnx_w, rwkv_lnx_b, rwkv_w_out):
    B, T, D = x.shape
    depth = norm_w.shape[0]
    c_pad = jnp.pad(c, ((0, 8 - B % 8 if B % 8 else 0), (0, 0)))
    v_first = None
    for i in range(depth):
        mod = _adaln_mod(c_pad, w_mod[i], b_mod[i])[:B]
        shift, scale, gate = mod[:, :D], mod[:, D:2 * D], mod[:, 2 * D:]
        h = _norm_mod(x, norm_w[i], scale, shift, BF16 if i % 2 == 0 else F32)
        j = i // 2
        if i % 2 == 0:
            x, v = _fox_layer(x, h, gate, fox_w_in[j], fox_b_f[j], fox_q_norm_w[j], fox_k_norm_w[j],
                              fox_w_out[j])
            if v_first is None:
                v_first = v
        else:
            x = _rwkv_layer(x, h, gate, v_first, rwkv_mu[j], rwkv_w_in[j], rwkv_w0[j], rwkv_w1[j],
                            rwkv_w2[j], rwkv_a0[j], rwkv_a1[j], rwkv_a2[j], rwkv_v0[j], rwkv_v1[j],
                            rwkv_v2[j], rwkv_k_k[j], rwkv_k_a[j], rwkv_r_k[j], rwkv_lnx_w[j],
                            rwkv_lnx_b[j], rwkv_w_out[j])
    return x
```

```python
import functools

import jax
import jax.numpy as jnp
from jax import lax
from jax.experimental import pallas as pl
from jax.experimental.pallas import tpu as pltpu

F32 = jnp.float32
BF16 = jnp.bfloat16

HEAD_DIM = 64
LANES = 128
MXU_DIM = 256
RWKV_CHUNK = 64
HEADS_PER_GROUP = MXU_DIM // HEAD_DIM
NORM_EPS = 1e-6
LNX_EPS = 64e-5
VMEM_LIMIT = 56 * 1024 * 1024
NEG_BIG = -1e30


def _tile(n, pref):
    t = min(n, pref)
    while n % t:
        t //= 2
    return t


def _norm_mod_kernel(x_ref, nw_ref, sc_ref, sh_ref, o_ref):
    x = x_ref[...]
    ms = jnp.mean(x * x, axis=-1, keepdims=True)
    y = x * lax.rsqrt(ms + NORM_EPS) * nw_ref[...]
    o_ref[...] = (y * (1.0 + sc_ref[...]) + sh_ref[...]).astype(o_ref.dtype)


def _norm_mod(x, norm_w, scale, shift, out_dtype):
    B, T, D = x.shape
    tm = _tile(T, 512)
    row = pl.BlockSpec((None, tm, D), lambda b, i: (b, i, 0))
    per_b = pl.BlockSpec((None, 1, D), lambda b, i: (b, 0, 0))
    return pl.pallas_call(
        _norm_mod_kernel,
        grid=(B, T // tm),
        in_specs=[row, pl.BlockSpec((1, D), lambda b, i: (0, 0)), per_b, per_b],
        out_specs=row,
        out_shape=jax.ShapeDtypeStruct((B, T, D), out_dtype),
        compiler_params=pltpu.CompilerParams(
            dimension_semantics=("parallel", "parallel"), vmem_limit_bytes=VMEM_LIMIT),
        name="norm_mod",
    )(x, norm_w.reshape(1, D), scale.reshape(B, 1, D), shift.reshape(B, 1, D))


def _matmul_kernel(a_ref, b_ref, o_ref):
    o_ref[...] = jnp.dot(a_ref[...], b_ref[...], preferred_element_type=F32).astype(o_ref.dtype)


def _matmul(a, b, out_dtype=F32):
    M, K = a.shape
    N = b.shape[1]
    tm, tn = _tile(M, 1024), _tile(N, 1024)
    return pl.pallas_call(
        _matmul_kernel,
        grid=(M // tm, N // tn),
        in_specs=[pl.BlockSpec((tm, K), lambda i, j: (i, 0)),
                  pl.BlockSpec((K, tn), lambda i, j: (0, j))],
        out_specs=pl.BlockSpec((tm, tn), lambda i, j: (i, j)),
        out_shape=jax.ShapeDtypeStruct((M, N), out_dtype),
        compiler_params=pltpu.CompilerParams(
            dimension_semantics=("parallel", "parallel"), vmem_limit_bytes=VMEM_LIMIT),
        name="matmul",
    )(a, b)


def _matmul_residual_kernel(a_ref, b_ref, x_ref, g_ref, o_ref):
    acc = jnp.dot(a_ref[...], b_ref[...], preferred_element_type=F32)
    o_ref[...] = x_ref[...] + g_ref[...] * acc


def _matmul_residual(a, b, x, gate):
    B, T, K = a.shape
    N = b.shape[1]
    tm, tn = _tile(T, 1024), _tile(N, 1024)
    return pl.pallas_call(
        _matmul_residual_kernel,
        grid=(B, T // tm, N // tn),
        in_specs=[pl.BlockSpec((None, tm, K), lambda bb, i, j: (bb, i, 0)),
                  pl.BlockSpec((K, tn), lambda bb, i, j: (0, j)),
                  pl.BlockSpec((None, tm, tn), lambda bb, i, j: (bb, i, j)),
                  pl.BlockSpec((None, 1, tn), lambda bb, i, j: (bb, 0, j))],
        out_specs=pl.BlockSpec((None, tm, tn), lambda bb, i, j: (bb, i, j)),
        out_shape=jax.ShapeDtypeStruct((B, T, N), F32),
        compiler_params=pltpu.CompilerParams(
            dimension_semantics=("parallel", "parallel", "parallel"), vmem_limit_bytes=VMEM_LIMIT),
        name="matmul_residual",
    )(a, b, x, gate.reshape(B, 1, N))


def _mod_kernel(c_ref, w_ref, b_ref, o_ref):
    c = c_ref[...]
    c_act = c * jax.nn.sigmoid(c)
    hi = c_act.astype(BF16)
    lo = (c_act - hi.astype(F32)).astype(BF16)
    w = w_ref[...]
    w_hi = w.astype(BF16)
    w_lo = (w - w_hi.astype(F32)).astype(BF16)
    acc = (jnp.dot(lo, w_hi, preferred_element_type=F32) + jnp.dot(hi, w_lo, preferred_element_type=F32)
           + jnp.dot(hi, w_hi, preferred_element_type=F32))
    o_ref[...] = acc + b_ref[...]


def _adaln_mod(c_pad, w_mod, b_mod):
    R, D = c_pad.shape
    N = w_mod.shape[1]
    tn = _tile(N, 512)
    return pl.pallas_call(
        _mod_kernel,
        grid=(N // tn,),
        in_specs=[pl.BlockSpec((R, D), lambda j: (0, 0)),
                  pl.BlockSpec((D, tn), lambda j: (0, j)),
                  pl.BlockSpec((1, tn), lambda j: (0, j))],
        out_specs=pl.BlockSpec((R, tn), lambda j: (0, j)),
        out_shape=jax.ShapeDtypeStruct((R, N), F32),
        compiler_params=pltpu.CompilerParams(
            dimension_semantics=("parallel",), vmem_limit_bytes=VMEM_LIMIT),
        name="adaln_mod",
    )(c_pad, w_mod, b_mod.reshape(1, N))


LOG2E = 1.4426950408889634


def _pair_rms(x, lo_half, w):
    ss = x * x
    s_lo = jnp.sum(jnp.where(lo_half, ss, 0.0), axis=1, keepdims=True)
    s_hi = jnp.sum(jnp.where(lo_half, 0.0, ss), axis=1, keepdims=True)
    ms = jnp.where(lo_half, s_lo, s_hi) * (1.0 / HEAD_DIM)
    return x * lax.rsqrt(ms + NORM_EPS) * w


def _fox_attn_kernel(q_ref, k_ref, v_ref, c_ref, z_ref, qw_ref, kw_ref, o_ref, kn_ref, va_ref, m_ref, acc_ref,
                     s_ref, *, tq, tb):
    qi = pl.program_id(2)
    T = k_ref.shape[0]

    @pl.when(qi == 0)
    def _():
        lane_b = lax.broadcasted_iota(jnp.int32, (tb, LANES), 1)
        lo_b = lane_b < HEAD_DIM

        def build(cidx, carry):
            rows = pl.ds(pl.multiple_of(cidx * tb, tb), tb)
            kn_ref[rows, :] = _pair_rms(k_ref[rows, :].astype(F32), lo_b, kw_ref[...]).astype(BF16)
            v2 = v_ref[rows, :].astype(F32)
            va_ref[0, rows, :] = jnp.where(lo_b, v2, jnp.where(lane_b == HEAD_DIM, 1.0, 0.0)).astype(BF16)
            va_ref[1, rows, :] = jnp.where(lo_b, jnp.where(lane_b == 0, 1.0, 0.0), v2).astype(BF16)
            return carry

        lax.fori_loop(0, T // tb, build, 0)

    lane = lax.broadcasted_iota(jnp.int32, (tq, LANES), 1)
    lo = lane < HEAD_DIM
    qn = _pair_rms(q_ref[...].astype(F32), lo, qw_ref[...])
    qh = (jnp.where(lo, qn, 0.0).astype(BF16), jnp.where(lo, 0.0, qn).astype(BF16))
    q_start = pl.multiple_of(qi * tq, tq)
    c_ref0 = [c_ref[h, :, pl.ds(q_start, LANES)][:, 0:1] for h in range(2)]
    tk = tq // 2
    col_minus_row = (lax.broadcasted_iota(jnp.int32, (tq, tk), 1)
                     - lax.broadcasted_iota(jnp.int32, (tq, tk), 0))
    nt = (((1,), (1,)), ((), ()))

    m_ref[...] = jnp.full(m_ref.shape, NEG_BIG, F32)
    acc_ref[...] = jnp.zeros(acc_ref.shape, F32)

    def scores(start, slot):
        start = pl.multiple_of(start, tk)
        kj = kn_ref[pl.ds(start, tk), :]
        for h in range(2):
            bias = (c_ref0[h] - c_ref[h, :, pl.ds(start, tk)]) * LOG2E
            s_ref[slot, h] = lax.dot_general(qh[h], kj, nt, preferred_element_type=F32) + bias

    def consume(start, slot, masked):
        start = pl.multiple_of(start, tk)
        for h in range(2):
            s = s_ref[slot, h]
            if masked:
                s = jnp.where(col_minus_row <= q_start - start, s, NEG_BIG)
            m = m_ref[h]
            m_new = jnp.maximum(m, jnp.max(s, axis=1, keepdims=True))
            p = jnp.exp2((s - m_new).astype(BF16))
            acc_ref[h] = jnp.exp2(m - m_new) * acc_ref[h] + jnp.dot(
                p, va_ref[h, pl.ds(start, tk), :], preferred_element_type=F32)
            m_ref[h] = m_new

    scores(0, 0)

    def body(i, carry):
        base = i * tq
        scores(base + tk, 1)
        consume(base, 0, False)
        scores(base + tq, 0)
        consume(base + tk, 1, False)
        return carry

    lax.fori_loop(0, qi, body, 0)
    scores(q_start + tk, 1)
    consume(q_start, 0, True)
    consume(q_start + tk, 1, True)
    acc0, acc1 = acc_ref[0], acc_ref[1]
    o = jnp.where(lo, acc0 / acc0[:, HEAD_DIM:HEAD_DIM + 1], acc1 / acc1[:, 0:1])
    z = z_ref[...].astype(F32)
    o_ref[...] = (o * (z * jax.nn.sigmoid(z))).astype(o_ref.dtype)


def _fox_attention(qk, v, cum_rows, z, q_norm_w, k_norm_w):
    B, T, D = v.shape
    H = D // HEAD_DIM
    tq = _tile(T, 1024)
    n_pair = D // LANES
    qw = jnp.tile(q_norm_w, 2).reshape(1, LANES) * (HEAD_DIM ** -0.5 * LOG2E)
    kw = jnp.tile(k_norm_w, 2).reshape(1, LANES)
    once = pl.Buffered(1)
    return pl.pallas_call(
        functools.partial(_fox_attn_kernel, tq=tq, tb=_tile(T, 512)),
        grid=(B, H // 2, T // tq),
        in_specs=[pl.BlockSpec((None, tq, LANES), lambda b, hp, i: (b, i, hp)),
                  pl.BlockSpec((None, T, LANES), lambda b, hp, i: (b, 0, n_pair + hp), pipeline_mode=once),
                  pl.BlockSpec((None, T, LANES), lambda b, hp, i: (b, 0, hp), pipeline_mode=once),
                  pl.BlockSpec((None, 2, 1, T), lambda b, hp, i: (b, hp, 0, 0), pipeline_mode=once),
                  pl.BlockSpec((None, tq, LANES), lambda b, hp, i: (b, i, hp)),
                  pl.BlockSpec((1, LANES), lambda b, hp, i: (0, 0)),
                  pl.BlockSpec((1, LANES), lambda b, hp, i: (0, 0))],
        out_specs=pl.BlockSpec((None, tq, LANES), lambda b, hp, i: (b, i, hp)),
        out_shape=jax.ShapeDtypeStruct((B, T, D), BF16),
        scratch_shapes=[pltpu.VMEM((T, LANES), BF16), pltpu.VMEM((2, T, LANES), BF16),
                        pltpu.VMEM((2, tq, 1), F32), pltpu.VMEM((2, tq, LANES), F32),
                        pltpu.VMEM((2, 2, tq, tq // 2), F32)],
        compiler_params=pltpu.CompilerParams(
            dimension_semantics=("parallel", "parallel", "arbitrary"), vmem_limit_bytes=VMEM_LIMIT),
        name="fox_attention",
    )(qk, qk, v, cum_rows, z, qw, kw)


def _trunc_bf16(x):
    bits = lax.bitcast_convert_type(x, jnp.uint32) & jnp.uint32(0xFFFF0000)
    return lax.bitcast_convert_type(bits, F32)


def _split3(x):
    hi = _trunc_bf16(x)
    r1 = x - hi
    mid = _trunc_bf16(r1)
    lo = r1 - mid
    return hi.astype(BF16), mid.astype(BF16), lo.astype(BF16)


_NN = (((1,), (0,)), ((), ()))
_NT = (((1,), (1,)), ((), ()))
_TN = (((0,), (0,)), ((), ()))


def _hl(x):
    hi = x.astype(BF16)
    return hi, (x - hi.astype(F32)).astype(BF16)


def _h(x):
    return (x.astype(BF16),)


def _mm3(a, b, dims=_NN):
    dg = functools.partial(lax.dot_general, dimension_numbers=dims, preferred_element_type=F32)
    if len(a) == 1 or len(b) == 1:
        return dg(a[0], b[0])
    return (dg(a[1], b[0]) + dg(a[0], b[1])) + dg(a[0], b[0])


def _rwkv_chunks(groups, head_masks):
    C, W = groups[0][0].shape
    n = range(len(groups))
    t_idx = lax.broadcasted_iota(jnp.int32, (C, W), 0)
    i_idx = lax.broadcasted_iota(jnp.int32, (C, W), 1) & (HEAD_DIM - 1)
    strict = i_idx < t_idx
    incl = i_idx <= t_idx
    eye = jnp.where(i_idx == t_idx, 1.0, 0.0)
    rw = lax.broadcasted_iota(jnp.int32, (W, W), 0) >> 6
    cw = lax.broadcasted_iota(jnp.int32, (W, W), 1) >> 6
    same_head = rw == cw

    def bd(parts):
        return tuple(jnp.concatenate([p * mk for mk in head_masks], axis=0) for p in parts)

    rr = lax.broadcasted_iota(jnp.int32, (C, C), 0)
    cc = lax.broadcasted_iota(jnp.int32, (C, C), 1)
    tri = jnp.where(cc <= rr, 1.0, 0.0).astype(BF16)
    mm = functools.partial(jnp.dot, preferred_element_type=F32)

    r, lw, k, v, a, b, S = [[grp[i] for grp in groups] for i in range(7)]
    lw3 = [_split3(lw[g]) for g in n]
    cum = [(mm(tri, lw3[g][2]) + mm(tri, lw3[g][1])) + mm(tri, lw3[g][0]) for g in n]
    e_neg = [jnp.exp(-cum[g]) for g in n]
    AR = [_h(jnp.concatenate([a[g] * jnp.exp(cum[g] - lw[g]), r[g] * jnp.exp(cum[g])], axis=0))
          for g in n]
    sB = [_mm3(AR[g], bd(_h(b[g] * e_neg[g])), _NT) for g in n]
    sK = [_mm3(AR[g], bd(_h(k[g] * e_neg[g])), _NT) for g in n]
    A_ab = [jnp.where(strict, sB[g][:C], 0.0) for g in n]

    P = [eye + A_ab[g] for g in n]
    Apow = [_hl(A_ab[g]) for g in n]
    for _ in range(max(C.bit_length() - 2, 0)):
        Apow = [_hl(_mm3(Apow[g], bd(Apow[g]))) for g in n]
        P = [P[g] + _mm3(Apow[g], bd(_hl(P[g]))) for g in n]

    A_k = [_h(jnp.concatenate([jnp.where(strict, sK[g][:C], 0.0), jnp.where(incl, sK[g][C:], 0.0)], axis=0))
           for g in n]
    XY = [_mm3(AR[g], _h(S[g]), _NT) + _mm3(A_k[g], bd(_h(v[g]))) for g in n]
    U = [_mm3(_hl(P[g]), bd(_hl(XY[g][:C]))) for g in n]
    Y = [XY[g][C:] + _mm3(_h(jnp.where(incl, sB[g][C:], 0.0)), bd(_h(U[g]))) for g in n]
    out = []
    for g in n:
        cum_end = cum[g][C - 1:C, :]
        e_end = jnp.exp(cum_end - cum[g])
        upd = _mm3(_h(jnp.concatenate([U[g], v[g]], axis=0)),
                   _h(jnp.concatenate([b[g] * e_end, k[g] * e_end], axis=0)), _TN)
        out.append((Y[g], S[g] * jnp.exp(cum_end) + jnp.where(same_head, upd, 0.0)))
    return out


_RWKV_ROW_INPUTS = 8
_RWKV_VEC_PARAMS = 8


def _rwkv_kernel(*refs):
    row_refs = refs[:_RWKV_ROW_INPUTS]
    vec_refs = refs[_RWKV_ROW_INPUTS:_RWKV_ROW_INPUTS + _RWKV_VEC_PARAMS]
    o_ref, s_ref = refs[_RWKV_ROW_INPUTS + _RWKV_VEC_PARAMS:]

    @pl.when(pl.program_id(2) == 0)
    def _():
        s_ref[...] = jnp.zeros_like(s_ref)

    C, W = row_refs[0].shape[0], MXU_DIM
    lane_head = lax.broadcasted_iota(jnp.int32, (C, W), 1) >> 6
    in_head = [lane_head == h for h in range(HEADS_PER_GROUP)]
    head_masks = [jnp.where(m, 1.0, 0.0).astype(BF16) for m in in_head]
    col_slices = [slice(g * W, (g + 1) * W) for g in range(row_refs[0].shape[1] // W)]
    n = range(len(col_slices))

    def head_sum(x):
        parts = [jnp.sum(jnp.where(m, x, 0.0), axis=1, keepdims=True) for m in in_head]
        out = parts[-1]
        for m, p in zip(in_head[-2::-1], parts[-2::-1]):
            out = jnp.where(m, p, out)
        return out

    rows = [[ref[:, cols].astype(F32) for ref in row_refs] for cols in col_slices]
    vecs = [[ref[:, cols] for ref in vec_refs] for cols in col_slices]
    groups, keep = [], []
    for g in n:
        r, k, v, wl, al, vl, vf, z = rows[g]
        w0, a0, v0, k_k, k_a, r_k, _, _ = vecs[g]
        t = -(w0 + wl)
        softplus = jnp.maximum(t, 0.0) + jnp.log(1.0 + jnp.exp(-jnp.abs(t)))
        lw = -jnp.exp(-softplus - 0.5)
        rate = jax.nn.sigmoid(a0 + al)
        vv = v + (vf - v) * jax.nn.sigmoid(v0 + vl)
        kk = k * k_k
        kk = kk / jnp.maximum(jnp.sqrt(head_sum(kk * kk)), 1e-12)
        k2 = k * (1.0 + (rate - 1.0) * k_a)
        groups.append((r, lw, k2, vv, -kk, kk * rate, s_ref[g]))
        keep.append((head_sum(r * k2 * r_k) * vv, z))
    results = _rwkv_chunks(groups, head_masks)
    for g, cols in enumerate(col_slices):
        y, s_new = results[g]
        bonus, z = keep[g]
        lnx_w, lnx_b = vecs[g][6], vecs[g][7]
        d = y - head_sum(y) * (1.0 / HEAD_DIM)
        var = head_sum(d * d) * (1.0 / HEAD_DIM)
        yn = d * lax.rsqrt(var + LNX_EPS) * lnx_w + lnx_b
        o_ref[:, cols] = ((yn + bonus) * (z * jax.nn.sigmoid(z))).astype(o_ref.dtype)
        s_ref[g] = s_new


def _rwkv_mix(r, k, v, wl, al, vl, v_first, z, vec_params):
    B, T, D = r.shape
    C, W = RWKV_CHUNK, MXU_DIM
    groups = next(g for g in (4, 2, 1) if D % (g * W) == 0)
    spec = pl.BlockSpec((None, C, groups * W), lambda bb, g, c: (bb, c, g))
    vec_spec = pl.BlockSpec((1, groups * W), lambda bb, g, c: (0, g))
    return pl.pallas_call(
        _rwkv_kernel,
        grid=(B, D // (groups * W), T // C),
        in_specs=[spec] * _RWKV_ROW_INPUTS + [vec_spec] * _RWKV_VEC_PARAMS,
        out_specs=spec,
        out_shape=jax.ShapeDtypeStruct((B, T, D), BF16),
        scratch_shapes=[pltpu.VMEM((groups, W, W), F32)],
        compiler_params=pltpu.CompilerParams(
            dimension_semantics=("parallel", "parallel", "arbitrary"), vmem_limit_bytes=VMEM_LIMIT),
        name="rwkv7_mix",
    )(r, k, v, wl, al, vl, v_first, z, *[p.reshape(1, D) for p in vec_params])


def _fox_layer(x, h, gate, w_in, b_f, q_norm_w, k_norm_w, w_out):
    B, T, D = x.shape
    H = D // HEAD_DIM
    M = B * T
    hb = h.reshape(M, D)
    w_b = w_in.astype(BF16)
    qk = _matmul(hb, w_b[:, :2 * D], BF16).reshape(B, T, 2 * D)
    v = _matmul(hb, w_b[:, 2 * D:3 * D], F32).reshape(B, T, D)
    z = _matmul(hb, w_b[:, 3 * D:4 * D], BF16).reshape(B, T, D)
    f_logit = _matmul(hb, _pad_cols(w_b[:, 4 * D:], LANES), F32)[:, :H].reshape(B, T, H) + b_f
    cum = jnp.cumsum(jax.nn.log_sigmoid(f_logit), axis=1)
    cum_rows = jnp.transpose(cum, (0, 2, 1)).reshape(B, H, 1, T)
    g = _fox_attention(qk, v.astype(BF16), cum_rows, z, q_norm_w, k_norm_w)
    return _matmul_residual(g, w_out.astype(BF16), x, gate), v


def _pad_cols(w, n):
    return jnp.pad(w, ((0, 0), (0, n - w.shape[1])))


def _pad_rows(w, n):
    return jnp.pad(w, ((0, n - w.shape[0]), (0, 0)))


def _lora(xb, w1, w2, act):
    rank = w1.shape[1]
    rp = -(-rank // LANES) * LANES
    mid = act(_matmul(xb, _pad_cols(w1, rp).astype(BF16)))
    return _matmul(mid.astype(BF16), _pad_rows(w2, rp).astype(BF16))


def _rwkv_layer(x, h, gate, v_first, mu, w_in, w0, w1, w2, a0, a1, a2, v0, v1, v2,
                k_k, k_a, r_k, lnx_w, lnx_b, w_out):
    B, T, D = x.shape
    M = B * T
    shp = (B, T, D)
    xx = jnp.pad(h[:, :-1], ((0, 0), (1, 0), (0, 0))) - h
    xr, xw, xk, xv, xa, xg = [(h + xx * mu[i]).reshape(M, D).astype(BF16) for i in range(6)]
    r = _matmul(xr, w_in[0].astype(BF16)).reshape(shp)
    k = _matmul(xk, w_in[1].astype(BF16)).reshape(shp)
    v = _matmul(xv, w_in[2].astype(BF16)).reshape(shp)
    z = _matmul(xg, w_in[3].astype(BF16), BF16).reshape(shp)
    wl = _lora(xw, w1, w2, jnp.tanh).reshape(shp)
    al = _lora(xa, a1, a2, lambda t: t).reshape(shp)
    vl = _lora(xv, v1, v2, lambda t: t).reshape(shp)
    g = _rwkv_mix(r, k, v, wl, al, vl, v_first, z,
                  (w0, a0, v0, k_k, k_a, r_k.reshape(D), lnx_w, lnx_b))
    return _matmul_residual(g, w_out.astype(BF16), x, gate)


def kernel(x, c, norm_w, w_mod, b_mod, fox_w_in, fox_b_f, fox_q_norm_w, fox_k_norm_w, fox_w_out, rwkv_mu, rwkv_w_in, rwkv_w0, rwkv_w1, rwkv_w2, rwkv_a0, rwkv_a1, rwkv_a2, rwkv_v0, rwkv_v1, rwkv_v2, rwkv_k_k, rwkv_k_a, rwkv_r_k, rwkv_lnx_w, rwkv_lnx_b, rwkv_w_out):
    B, T, D = x.shape
    depth = norm_w.shape[0]
    c_pad = jnp.pad(c, ((0, 8 - B % 8 if B % 8 else 0), (0, 0)))
    v_first = None
    for i in range(depth):
        mod = _adaln_mod(c_pad, w_mod[i], b_mod[i])[:B]
        shift, scale, gate = mod[:, :D], mod[:, D:2 * D], mod[:, 2 * D:]
        h = _norm_mod(x, norm_w[i], scale, shift, BF16 if i % 2 == 0 else F32)
        j = i // 2
        if i % 2 == 0:
            x, v = _fox_layer(x, h, gate, fox_w_in[j], fox_b_f[j], fox_q_norm_w[j], fox_k_norm_w[j],
                              fox_w_out[j])
            if v_first is None:
                v_first = v
        else:
            x = _rwkv_layer(x, h, gate, v_first, rwkv_mu[j], rwkv_w_in[j], rwkv_w0[j], rwkv_w1[j],
                            rwkv_w2[j], rwkv_a0[j], rwkv_a1[j], rwkv_a2[j], rwkv_v0[j], rwkv_v1[j],
                            rwkv_v2[j], rwkv_k_k[j], rwkv_k_a[j], rwkv_r_k[j], rwkv_lnx_w[j],
                            rwkv_lnx_b[j], rwkv_w_out[j])
    return x
```

```python
import functools

import jax
import jax.numpy as jnp
from jax import lax
from jax.experimental import pallas as pl
from jax.experimental.pallas import tpu as pltpu

F32 = jnp.float32
BF16 = jnp.bfloat16

HEAD_DIM = 64
LANES = 128
MXU_DIM = 256
RWKV_CHUNK = 64
HEADS_PER_GROUP = MXU_DIM // HEAD_DIM
NORM_EPS = 1e-6
LNX_EPS = 64e-5
VMEM_LIMIT = 56 * 1024 * 1024
NEG_BIG = -1e30


def _tile(n, pref):
    t = min(n, pref)
    while n % t:
        t //= 2
    return t


def _norm_mod_kernel(x_ref, nw_ref, sc_ref, sh_ref, o_ref):
    x = x_ref[...]
    ms = jnp.mean(x * x, axis=-1, keepdims=True)
    y = x * lax.rsqrt(ms + NORM_EPS) * nw_ref[...]
    o_ref[...] = (y * (1.0 + sc_ref[...]) + sh_ref[...]).astype(o_ref.dtype)


def _norm_mod(x, norm_w, scale, shift, out_dtype):
    B, T, D = x.shape
    tm = _tile(T, 512)
    row = pl.BlockSpec((None, tm, D), lambda b, i: (b, i, 0))
    per_b = pl.BlockSpec((None, 1, D), lambda b, i: (b, 0, 0))
    return pl.pallas_call(
        _norm_mod_kernel,
        grid=(B, T // tm),
        in_specs=[row, pl.BlockSpec((1, D), lambda b, i: (0, 0)), per_b, per_b],
        out_specs=row,
        out_shape=jax.ShapeDtypeStruct((B, T, D), out_dtype),
        compiler_params=pltpu.CompilerParams(
            dimension_semantics=("parallel", "parallel"), vmem_limit_bytes=VMEM_LIMIT),
        name="norm_mod",
    )(x, norm_w.reshape(1, D), scale.reshape(B, 1, D), shift.reshape(B, 1, D))


def _matmul_kernel(a_ref, b_ref, o_ref):
    o_ref[...] = jnp.dot(a_ref[...], b_ref[...], preferred_element_type=F32).astype(o_ref.dtype)


def _matmul(a, b, out_dtype=F32):
    M, K = a.shape
    N = b.shape[1]
    tm, tn = _tile(M, 1024), _tile(N, 1024)
    return pl.pallas_call(
        _matmul_kernel,
        grid=(M // tm, N // tn),
        in_specs=[pl.BlockSpec((tm, K), lambda i, j: (i, 0)),
                  pl.BlockSpec((K, tn), lambda i, j: (0, j))],
        out_specs=pl.BlockSpec((tm, tn), lambda i, j: (i, j)),
        out_shape=jax.ShapeDtypeStruct((M, N), out_dtype),
        compiler_params=pltpu.CompilerParams(
            dimension_semantics=("parallel", "parallel"), vmem_limit_bytes=VMEM_LIMIT),
        name="matmul",
    )(a, b)


def _matmul_residual_kernel(a_ref, b_ref, x_ref, g_ref, o_ref):
    acc = jnp.dot(a_ref[...], b_ref[...], preferred_element_type=F32)
    o_ref[...] = x_ref[...] + g_ref[...] * acc


def _matmul_residual(a, b, x, gate):
    B, T, K = a.shape
    N = b.shape[1]
    tm, tn = _tile(T, 1024), _tile(N, 1024)
    return pl.pallas_call(
        _matmul_residual_kernel,
        grid=(B, T // tm, N // tn),
        in_specs=[pl.BlockSpec((None, tm, K), lambda bb, i, j: (bb, i, 0)),
                  pl.BlockSpec((K, tn), lambda bb, i, j: (0, j)),
                  pl.BlockSpec((None, tm, tn), lambda bb, i, j: (bb, i, j)),
                  pl.BlockSpec((None, 1, tn), lambda bb, i, j: (bb, 0, j))],
        out_specs=pl.BlockSpec((None, tm, tn), lambda bb, i, j: (bb, i, j)),
        out_shape=jax.ShapeDtypeStruct((B, T, N), F32),
        compiler_params=pltpu.CompilerParams(
            dimension_semantics=("parallel", "parallel", "parallel"), vmem_limit_bytes=VMEM_LIMIT),
        name="matmul_residual",
    )(a, b, x, gate.reshape(B, 1, N))


def _mod_kernel(c_ref, w_ref, b_ref, o_ref):
    c = c_ref[...]
    c_act = c * jax.nn.sigmoid(c)
    hi = c_act.astype(BF16)
    lo = (c_act - hi.astype(F32)).astype(BF16)
    w = w_ref[...]
    w_hi = w.astype(BF16)
    w_lo = (w - w_hi.astype(F32)).astype(BF16)
    acc = (jnp.dot(lo, w_hi, preferred_element_type=F32) + jnp.dot(hi, w_lo, preferred_element_type=F32)
           + jnp.dot(hi, w_hi, preferred_element_type=F32))
    o_ref[...] = acc + b_ref[...]


def _adaln_mod(c_pad, w_mod, b_mod):
    R, D = c_pad.shape
    N = w_mod.shape[1]
    tn = _tile(N, 512)
    return pl.pallas_call(
        _mod_kernel,
        grid=(N // tn,),
        in_specs=[pl.BlockSpec((R, D), lambda j: (0, 0)),
                  pl.BlockSpec((D, tn), lambda j: (0, j)),
                  pl.BlockSpec((1, tn), lambda j: (0, j))],
        out_specs=pl.BlockSpec((R, tn), lambda j: (0, j)),
        out_shape=jax.ShapeDtypeStruct((R, N), F32),
        compiler_params=pltpu.CompilerParams(
            dimension_semantics=("parallel",), vmem_limit_bytes=VMEM_LIMIT),
        name="adaln_mod",
    )(c_pad, w_mod, b_mod.reshape(1, N))


LOG2E = 1.4426950408889634
PV_PAD = 16


def _pair_rms(x, lo_half, w):
    ss = x * x
    s_lo = jnp.sum(jnp.where(lo_half, ss, 0.0), axis=1, keepdims=True)
    s_hi = jnp.sum(jnp.where(lo_half, 0.0, ss), axis=1, keepdims=True)
    ms = jnp.where(lo_half, s_lo, s_hi) * (1.0 / HEAD_DIM)
    return x * lax.rsqrt(ms + NORM_EPS) * w


def _fox_attn_kernel(q_ref, k_ref, vt_ref, c_ref, z_ref, qw_ref, kw_ref, o_ref, kn_ref, vat_ref, m_ref, acc_ref,
                     s_ref, *, tq, tb):
    qi = pl.program_id(2)
    T = k_ref.shape[0]

    @pl.when(qi == 0)
    def _():
        lo_b = lax.broadcasted_iota(jnp.int32, (tb, LANES), 1) < HEAD_DIM
        row_b = lax.broadcasted_iota(jnp.int32, (LANES, tb), 0)
        top_b = row_b < HEAD_DIM

        def build(cidx, carry):
            blk = pl.ds(pl.multiple_of(cidx * tb, tb), tb)
            kn_ref[blk, :] = _pair_rms(k_ref[blk, :].astype(F32), lo_b, kw_ref[...]).astype(BF16)
            vt = vt_ref[:, blk].astype(F32)
            vat_ref[0, :, blk] = jnp.where(top_b, vt, jnp.where(row_b == HEAD_DIM, 1.0, 0.0)).astype(BF16)
            vat_ref[1, :, blk] = jnp.where(top_b, jnp.where(row_b == HEAD_DIM - 1, 1.0, 0.0), vt).astype(BF16)
            return carry

        lax.fori_loop(0, T // tb, build, 0)

    lo = lax.broadcasted_iota(jnp.int32, (tq, LANES), 1) < HEAD_DIM
    qn = _pair_rms(q_ref[...].astype(F32), lo, qw_ref[...])
    qh = (jnp.where(lo, qn, 0.0).astype(BF16), jnp.where(lo, 0.0, qn).astype(BF16))
    q_start = pl.multiple_of(qi * tq, tq)
    c_ref0 = [c_ref[h, :, pl.ds(q_start, LANES)][:, 0:1] for h in range(2)]
    tk = tq // 2
    row_minus_col = (lax.broadcasted_iota(jnp.int32, (tk, tq), 0)
                     - lax.broadcasted_iota(jnp.int32, (tk, tq), 1))
    nt = (((1,), (1,)), ((), ()))
    va_rows = (slice(0, HEAD_DIM + PV_PAD), slice(HEAD_DIM - PV_PAD, LANES))

    m_ref[...] = jnp.full(m_ref.shape, NEG_BIG, F32)
    acc_ref[...] = jnp.zeros(acc_ref.shape, F32)

    def scores(start, slot):
        start = pl.multiple_of(start, tk)
        kj = kn_ref[pl.ds(start, tk), :]
        for h in range(2):
            bias = (c_ref0[h] - c_ref[h, :, pl.ds(start, tk)]) * LOG2E
            bias_col = jnp.transpose(jnp.broadcast_to(bias, (8, tk)))[:, 0:1]
            s_ref[slot, h] = lax.dot_general(kj, qh[h], nt, preferred_element_type=F32) + bias_col

    def consume(start, slot, masked):
        start = pl.multiple_of(start, tk)
        for h in range(2):
            s = s_ref[slot, h]
            if masked:
                s = jnp.where(row_minus_col <= q_start - start, s, NEG_BIG)
            m = m_ref[h]
            m_new = jnp.maximum(m, jnp.max(s, axis=0, keepdims=True))
            p = jnp.exp2((s - m_new).astype(BF16))
            acc_ref[h] = jnp.exp2(m - m_new) * acc_ref[h] + jnp.dot(
                vat_ref[h, va_rows[h], pl.ds(start, tk)], p, preferred_element_type=F32)
            m_ref[h] = m_new

    scores(0, 0)

    def body(i, carry):
        base = i * tq
        scores(base + tk, 1)
        consume(base, 0, False)
        scores(base + tq, 0)
        consume(base + tk, 1, False)
        return carry

    lax.fori_loop(0, qi, body, 0)
    scores(q_start + tk, 1)
    consume(q_start, 0, True)
    consume(q_start + tk, 1, True)
    acc0, acc1 = acc_ref[0], acc_ref[1]
    o_t = jnp.concatenate([acc0[:HEAD_DIM] / acc0[HEAD_DIM:HEAD_DIM + 1, :],
                           acc1[PV_PAD:] / acc1[PV_PAD - 1:PV_PAD, :]], axis=0)
    z = z_ref[...].astype(F32)
    o_ref[...] = (jnp.transpose(o_t) * (z * jax.nn.sigmoid(z))).astype(o_ref.dtype)


def _fox_attention(qk, v_t, cum_rows, z, q_norm_w, k_norm_w):
    B, D, T = v_t.shape
    H = D // HEAD_DIM
    tq = _tile(T, 1024)
    n_pair = D // LANES
    qw = jnp.tile(q_norm_w, 2).reshape(1, LANES) * (HEAD_DIM ** -0.5 * LOG2E)
    kw = jnp.tile(k_norm_w, 2).reshape(1, LANES)
    once = pl.Buffered(1)
    return pl.pallas_call(
        functools.partial(_fox_attn_kernel, tq=tq, tb=_tile(T, 512)),
        grid=(B, H // 2, T // tq),
        in_specs=[pl.BlockSpec((None, tq, LANES), lambda b, hp, i: (b, i, hp)),
                  pl.BlockSpec((None, T, LANES), lambda b, hp, i: (b, 0, n_pair + hp), pipeline_mode=once),
                  pl.BlockSpec((None, LANES, T), lambda b, hp, i: (b, hp, 0), pipeline_mode=once),
                  pl.BlockSpec((None, 2, 1, T), lambda b, hp, i: (b, hp, 0, 0), pipeline_mode=once),
                  pl.BlockSpec((None, tq, LANES), lambda b, hp, i: (b, i, hp)),
                  pl.BlockSpec((1, LANES), lambda b, hp, i: (0, 0)),
                  pl.BlockSpec((1, LANES), lambda b, hp, i: (0, 0))],
        out_specs=pl.BlockSpec((None, tq, LANES), lambda b, hp, i: (b, i, hp)),
        out_shape=jax.ShapeDtypeStruct((B, T, D), BF16),
        scratch_shapes=[pltpu.VMEM((T, LANES), BF16), pltpu.VMEM((2, LANES, T), BF16),
                        pltpu.VMEM((2, 1, tq), F32), pltpu.VMEM((2, HEAD_DIM + PV_PAD, tq), F32),
                        pltpu.VMEM((2, 2, tq // 2, tq), F32)],
        compiler_params=pltpu.CompilerParams(
            dimension_semantics=("parallel", "parallel", "arbitrary"), vmem_limit_bytes=VMEM_LIMIT),
        name="fox_attention",
    )(qk, qk, v_t, cum_rows, z, qw, kw)


def _trunc_bf16(x):
    bits = lax.bitcast_convert_type(x, jnp.uint32) & jnp.uint32(0xFFFF0000)
    return lax.bitcast_convert_type(bits, F32)


def _split3(x):
    hi = _trunc_bf16(x)
    r1 = x - hi
    mid = _trunc_bf16(r1)
    lo = r1 - mid
    return hi.astype(BF16), mid.astype(BF16), lo.astype(BF16)


_NN = (((1,), (0,)), ((), ()))
_NT = (((1,), (1,)), ((), ()))
_TN = (((0,), (0,)), ((), ()))


def _hl(x):
    hi = x.astype(BF16)
    return hi, (x - hi.astype(F32)).astype(BF16)


def _h(x):
    return (x.astype(BF16),)


def _mm3(a, b, dims=_NN):
    dg = functools.partial(lax.dot_general, dimension_numbers=dims, preferred_element_type=F32)
    if len(a) == 1 or len(b) == 1:
        return dg(a[0], b[0])
    return (dg(a[1], b[0]) + dg(a[0], b[1])) + dg(a[0], b[0])


def _rwkv_chunks(groups, head_masks):
    C, W = groups[0][0].shape
    n = range(len(groups))
    t_idx = lax.broadcasted_iota(jnp.int32, (C, W), 0)
    i_idx = lax.broadcasted_iota(jnp.int32, (C, W), 1) & (HEAD_DIM - 1)
    strict = i_idx < t_idx
    incl = i_idx <= t_idx
    eye = jnp.where(i_idx == t_idx, 1.0, 0.0)
    rw = lax.broadcasted_iota(jnp.int32, (W, W), 0) >> 6
    cw = lax.broadcasted_iota(jnp.int32, (W, W), 1) >> 6
    same_head = rw == cw

    def bd(parts):
        return tuple(jnp.concatenate([jnp.where(mk, p, jnp.zeros_like(p)) for mk in head_masks], axis=0)
                     for p in parts)

    rr = lax.broadcasted_iota(jnp.int32, (C, C), 0)
    cc = lax.broadcasted_iota(jnp.int32, (C, C), 1)
    tri = jnp.where(cc <= rr, 1.0, 0.0).astype(BF16)
    mm = functools.partial(jnp.dot, preferred_element_type=F32)

    r, lw, k, v, a, b, S = [[grp[i] for grp in groups] for i in range(7)]
    lw3 = [_split3(lw[g]) for g in n]
    cum = [(mm(tri, lw3[g][2]) + mm(tri, lw3[g][1])) + mm(tri, lw3[g][0]) for g in n]
    e_neg = [jnp.exp(-cum[g]) for g in n]
    AR = [_h(jnp.concatenate([a[g] * jnp.exp(cum[g] - lw[g]), r[g] * jnp.exp(cum[g])], axis=0))
          for g in n]
    sB = [_mm3(AR[g], bd(_h(b[g] * e_neg[g])), _NT) for g in n]
    sK = [_mm3(AR[g], bd(_h(k[g] * e_neg[g])), _NT) for g in n]
    A_ab = [jnp.where(strict, sB[g][:C], 0.0) for g in n]

    P = [eye + A_ab[g] for g in n]
    Apow = [_hl(A_ab[g]) for g in n]
    assert C == 2 ** (_INV_SQUARINGS + 1)
    for _ in range(_INV_SQUARINGS):
        Apow = [_hl(_mm3(Apow[g], bd(Apow[g]))) for g in n]
        P = [P[g] + _mm3(Apow[g], bd(_hl(P[g]))) for g in n]

    A_k = [_h(jnp.concatenate([jnp.where(strict, sK[g][:C], 0.0), jnp.where(incl, sK[g][C:], 0.0)], axis=0))
           for g in n]
    XY = [_mm3(AR[g], _h(S[g]), _NT) + _mm3(A_k[g], bd(_h(v[g]))) for g in n]
    U = [_mm3(_hl(P[g]), bd(_hl(XY[g][:C]))) for g in n]
    Y = [XY[g][C:] + _mm3(_h(jnp.where(incl, sB[g][C:], 0.0)), bd(_h(U[g]))) for g in n]
    out = []
    for g in n:
        cum_end = cum[g][C - 1:C, :]
        e_end = jnp.exp(cum_end - cum[g])
        upd = _mm3(_h(jnp.concatenate([U[g], v[g]], axis=0)),
                   _h(jnp.concatenate([b[g] * e_end, k[g] * e_end], axis=0)), _TN)
        out.append((Y[g], S[g] * jnp.exp(cum_end) + jnp.where(same_head, upd, 0.0)))
    return out


_INV_SQUARINGS = 5
_RWKV_ROW_INPUTS = 8
_RWKV_VEC_PARAMS = 8


def _rwkv_kernel(*refs):
    row_refs = refs[:_RWKV_ROW_INPUTS]
    vec_refs = refs[_RWKV_ROW_INPUTS:_RWKV_ROW_INPUTS + _RWKV_VEC_PARAMS]
    o_ref, s_ref = refs[_RWKV_ROW_INPUTS + _RWKV_VEC_PARAMS:]

    @pl.when(pl.program_id(2) == 0)
    def _():
        s_ref[...] = jnp.zeros_like(s_ref)

    C, W = row_refs[0].shape[0], MXU_DIM
    lane_head = lax.broadcasted_iota(jnp.int32, (C, W), 1) >> 6
    in_head = [lane_head == h for h in range(HEADS_PER_GROUP)]
    col_slices = [slice(g * W, (g + 1) * W) for g in range(row_refs[0].shape[1] // W)]
    n = range(len(col_slices))

    def head_sum(x):
        parts = [jnp.sum(jnp.where(m, x, 0.0), axis=1, keepdims=True) for m in in_head]
        out = parts[-1]
        for m, p in zip(in_head[-2::-1], parts[-2::-1]):
            out = jnp.where(m, p, out)
        return out

    rows = [[ref[:, cols].astype(F32) for ref in row_refs] for cols in col_slices]
    vecs = [[ref[:, cols] for ref in vec_refs] for cols in col_slices]
    groups, keep = [], []
    for g in n:
        r, k, v, wl, al, vl, vf, z = rows[g]
        w0, a0, v0, k_k, k_a, r_k, _, _ = vecs[g]
        t = -(w0 + wl)
        softplus = jnp.maximum(t, 0.0) + jnp.log(1.0 + jnp.exp(-jnp.abs(t)))
        lw = -jnp.exp(-softplus - 0.5)
        rate = jax.nn.sigmoid(a0 + al)
        vv = v + (vf - v) * jax.nn.sigmoid(v0 + vl)
        kk = k * k_k
        kk = kk / jnp.maximum(jnp.sqrt(head_sum(kk * kk)), 1e-12)
        k2 = k * (1.0 + (rate - 1.0) * k_a)
        groups.append((r, lw, k2, vv, -kk, kk * rate, s_ref[g]))
        keep.append((head_sum(r * k2 * r_k) * vv, z))
    results = _rwkv_chunks(groups, in_head)
    for g, cols in enumerate(col_slices):
        y, s_new = results[g]
        bonus, z = keep[g]
        lnx_w, lnx_b = vecs[g][6], vecs[g][7]
        d = y - head_sum(y) * (1.0 / HEAD_DIM)
        var = head_sum(d * d) * (1.0 / HEAD_DIM)
        yn = d * lax.rsqrt(var + LNX_EPS) * lnx_w + lnx_b
        o_ref[:, cols] = ((yn + bonus) * (z * jax.nn.sigmoid(z))).astype(o_ref.dtype)
        s_ref[g] = s_new


def _rwkv_mix(r, k, v, wl, al, vl, v_first, z, vec_params):
    B, T, D = r.shape
    C, W = RWKV_CHUNK, MXU_DIM
    groups = next(g for g in (4, 2, 1) if D % (g * W) == 0)
    spec = pl.BlockSpec((None, C, groups * W), lambda bb, g, c: (bb, c, g))
    vec_spec = pl.BlockSpec((1, groups * W), lambda bb, g, c: (0, g))
    return pl.pallas_call(
        _rwkv_kernel,
        grid=(B, D // (groups * W), T // C),
        in_specs=[spec] * _RWKV_ROW_INPUTS + [vec_spec] * _RWKV_VEC_PARAMS,
        out_specs=spec,
        out_shape=jax.ShapeDtypeStruct((B, T, D), BF16),
        scratch_shapes=[pltpu.VMEM((groups, W, W), F32)],
        compiler_params=pltpu.CompilerParams(
            dimension_semantics=("parallel", "parallel", "arbitrary"), vmem_limit_bytes=VMEM_LIMIT),
        name="rwkv7_mix",
    )(r, k, v, wl, al, vl, v_first, z, *[p.reshape(1, D) for p in vec_params])


def _fox_layer(x, h, gate, w_in, b_f, q_norm_w, k_norm_w, w_out):
    B, T, D = x.shape
    H = D // HEAD_DIM
    M = B * T
    hb = h.reshape(M, D)
    w_b = w_in.astype(BF16)
    qk = _matmul(hb, w_b[:, :2 * D], BF16).reshape(B, T, 2 * D)
    v = _matmul(hb, w_b[:, 2 * D:3 * D], F32).reshape(B, T, D)
    z = _matmul(hb, w_b[:, 3 * D:4 * D], BF16).reshape(B, T, D)
    f_logit = _matmul(hb, _pad_cols(w_b[:, 4 * D:], LANES), F32)[:, :H].reshape(B, T, H) + b_f
    cum = jnp.cumsum(jax.nn.log_sigmoid(f_logit), axis=1)
    cum_rows = jnp.transpose(cum, (0, 2, 1)).reshape(B, H, 1, T)
    g = _fox_attention(qk, jnp.transpose(v.astype(BF16), (0, 2, 1)), cum_rows, z, q_norm_w, k_norm_w)
    return _matmul_residual(g, w_out.astype(BF16), x, gate), v


def _pad_cols(w, n):
    return jnp.pad(w, ((0, 0), (0, n - w.shape[1])))


def _pad_rows(w, n):
    return jnp.pad(w, ((0, n - w.shape[0]), (0, 0)))


def _lora(xb, w1, w2, act):
    rank = w1.shape[1]
    rp = -(-rank // LANES) * LANES
    mid = act(_matmul(xb, _pad_cols(w1, rp).astype(BF16)))
    return _matmul(mid.astype(BF16), _pad_rows(w2, rp).astype(BF16))


def _rwkv_layer(x, h, gate, v_first, mu, w_in, w0, w1, w2, a0, a1, a2, v0, v1, v2,
                k_k, k_a, r_k, lnx_w, lnx_b, w_out):
    B, T, D = x.shape
    M = B * T
    shp = (B, T, D)
    xx = jnp.pad(h[:, :-1], ((0, 0), (1, 0), (0, 0))) - h
    xr, xw, xk, xv, xa, xg = [(h + xx * mu[i]).reshape(M, D).astype(BF16) for i in range(6)]
    r = _matmul(xr, w_in[0].astype(BF16)).reshape(shp)
    k = _matmul(xk, w_in[1].astype(BF16)).reshape(shp)
    v = _matmul(xv, w_in[2].astype(BF16)).reshape(shp)
    z = _matmul(xg, w_in[3].astype(BF16), BF16).reshape(shp)
    wl = _lora(xw, w1, w2, jnp.tanh).reshape(shp)
    al = _lora(xa, a1, a2, lambda t: t).reshape(shp)
    vl = _lora(xv, v1, v2, lambda t: t).reshape(shp)
    g = _rwkv_mix(r, k, v, wl, al, vl, v_first, z,
                  (w0, a0, v0, k_k, k_a, r_k.reshape(D), lnx_w, lnx_b))
    return _matmul_residual(g, w_out.astype(BF16), x, gate)


def kernel(x, c, norm_w, w_mod, b_mod, fox_w_in, fox_b_f, fox_q_norm_w, fox_k_norm_w, fox_w_out, rwkv_mu, rwkv_w_in, rwkv_w0, rwkv_w1, rwkv_w2, rwkv_a0, rwkv_a1, rwkv_a2, rwkv_v0, rwkv_v1, rwkv_v2, rwkv_k_k, rwkv_k_a, rwkv_r_k, rwkv_lnx_w, rwkv_lnx_b, rwkv_w_out):
    B, T, D = x.shape
    depth = norm_w.shape[0]
    c_pad = jnp.pad(c, ((0, 8 - B % 8 if B % 8 else 0), (0, 0)))
    v_first = None
    for i in range(depth):
        mod = _adaln_mod(c_pad, w_mod[i], b_mod[i])[:B]
        shift, scale, gate = mod[:, :D], mod[:, D:2 * D], mod[:, 2 * D:]
        h = _norm_mod(x, norm_w[i], scale, shift, BF16 if i % 2 == 0 else F32)
        j = i // 2
        if i % 2 == 0:
            x, v = _fox_layer(x, h, gate, fox_w_in[j], fox_b_f[j], fox_q_norm_w[j], fox_k_norm_w[j],
                              fox_w_out[j])
            if v_first is None:
                v_first = v
        else:
            x = _rwkv_layer(x, h, gate, v_first, rwkv_mu[j], rwkv_w_in[j], rwkv_w0[j], rwkv_w1[j],
                            rwkv_w2[j], rwkv_a0[j], rwkv_a1[j], rwkv_a2[j], rwkv_v0[j], rwkv_v1[j],
                            rwkv_v2[j], rwkv_k_k[j], rwkv_k_a[j], rwkv_r_k[j], rwkv_lnx_w[j],
                            rwkv_lnx_b[j], rwkv_w_out[j])
    return x
```

```python
import functools

import jax
import jax.numpy as jnp
from jax import lax
from jax.experimental import pallas as pl
from jax.experimental.pallas import tpu as pltpu

F32 = jnp.float32
BF16 = jnp.bfloat16

HEAD_DIM = 64
LANES = 128
MXU_DIM = 256
RWKV_CHUNK = 64
INV_SQUARINGS = 5
HEADS_PER_GROUP = MXU_DIM // HEAD_DIM
NORM_EPS = 1e-6
LNX_EPS = 64e-5
VMEM_LIMIT = 56 * 1024 * 1024
NEG_BIG = -1e30


def _tile(n, pref):
    t = min(n, pref)
    while n % t:
        t //= 2
    return t


def _norm_mod_kernel(x_ref, nw_ref, sc_ref, sh_ref, o_ref):
    x = x_ref[...]
    ms = jnp.mean(x * x, axis=-1, keepdims=True)
    y = x * lax.rsqrt(ms + NORM_EPS) * nw_ref[...]
    o_ref[...] = (y * (1.0 + sc_ref[...]) + sh_ref[...]).astype(o_ref.dtype)


def _norm_mod(x, norm_w, scale, shift):
    B, T, D = x.shape
    tm = _tile(T, 512)
    row = pl.BlockSpec((None, tm, D), lambda b, i: (b, i, 0))
    per_b = pl.BlockSpec((None, 1, D), lambda b, i: (b, 0, 0))
    return pl.pallas_call(
        _norm_mod_kernel,
        grid=(B, T // tm),
        in_specs=[row, pl.BlockSpec((1, D), lambda b, i: (0, 0)), per_b, per_b],
        out_specs=row,
        out_shape=jax.ShapeDtypeStruct((B, T, D), BF16),
        compiler_params=pltpu.CompilerParams(
            dimension_semantics=("parallel", "parallel"), vmem_limit_bytes=VMEM_LIMIT),
        name="norm_mod",
    )(x, norm_w.reshape(1, D), scale.reshape(B, 1, D), shift.reshape(B, 1, D))


def _norm_mod_mix_kernel(x_ref, xp_ref, nw_ref, sc_ref, sh_ref, mu_ref, *o_refs):
    def modulated(x):
        ms = jnp.mean(x * x, axis=-1, keepdims=True)
        return x * lax.rsqrt(ms + NORM_EPS) * nw_ref[...] * (1.0 + sc_ref[...]) + sh_ref[...]

    h = modulated(x_ref[...])
    above = modulated(xp_ref[...])[7:8, :]
    above = jnp.where(pl.program_id(1) == 0, 0.0, above)
    row = lax.broadcasted_iota(jnp.int32, h.shape, 0)
    xx = jnp.where(row == 0, above, pltpu.roll(h, 1, axis=0)) - h
    for i, o_ref in enumerate(o_refs):
        o_ref[...] = (h + xx * mu_ref[i:i + 1, :]).astype(o_ref.dtype)


def _norm_mod_mix(x, norm_w, scale, shift, mu):
    B, T, D = x.shape
    n = mu.shape[0]
    tm = _tile(T, 512)
    row = pl.BlockSpec((None, tm, D), lambda b, i: (b, i, 0))
    prev = pl.BlockSpec((None, 8, D), lambda b, i: (b, jnp.maximum(i * (tm // 8) - 1, 0), 0))
    per_b = pl.BlockSpec((None, 1, D), lambda b, i: (b, 0, 0))
    return pl.pallas_call(
        _norm_mod_mix_kernel,
        grid=(B, T // tm),
        in_specs=[row, prev, pl.BlockSpec((1, D), lambda b, i: (0, 0)), per_b, per_b,
                  pl.BlockSpec((n, D), lambda b, i: (0, 0))],
        out_specs=[row] * n,
        out_shape=[jax.ShapeDtypeStruct((B, T, D), BF16)] * n,
        compiler_params=pltpu.CompilerParams(
            dimension_semantics=("parallel", "parallel"), vmem_limit_bytes=VMEM_LIMIT),
        name="norm_mod_mix",
    )(x, x, norm_w.reshape(1, D), scale.reshape(B, 1, D), shift.reshape(B, 1, D), mu)


def _matmul_kernel(a_ref, b_ref, o_ref):
    o_ref[...] = jnp.dot(a_ref[...], b_ref[...], preferred_element_type=F32).astype(o_ref.dtype)


def _matmul(a, b, out_dtype=F32):
    M, K = a.shape
    N = b.shape[1]
    tm, tn = _tile(M, 1024), _tile(N, 1024)
    return pl.pallas_call(
        _matmul_kernel,
        grid=(M // tm, N // tn),
        in_specs=[pl.BlockSpec((tm, K), lambda i, j: (i, 0)),
                  pl.BlockSpec((K, tn), lambda i, j: (0, j))],
        out_specs=pl.BlockSpec((tm, tn), lambda i, j: (i, j)),
        out_shape=jax.ShapeDtypeStruct((M, N), out_dtype),
        compiler_params=pltpu.CompilerParams(
            dimension_semantics=("parallel", "parallel"), vmem_limit_bytes=VMEM_LIMIT),
        name="matmul",
    )(a, b)


def _matmul_t_kernel(a_ref, b_ref, o_ref, ot_ref):
    acc = jnp.dot(a_ref[...], b_ref[...], preferred_element_type=F32)
    o_ref[...] = acc
    ot_ref[...] = jnp.transpose(acc).astype(ot_ref.dtype)


def _matmul_and_transposed(a, b, batch):
    M, K = a.shape
    N = b.shape[1]
    T = M // batch
    tm, tn = _tile(T, 1024), _tile(N, 1024)
    tiles_per_batch = T // tm
    return pl.pallas_call(
        _matmul_t_kernel,
        grid=(M // tm, N // tn),
        in_specs=[pl.BlockSpec((tm, K), lambda i, j: (i, 0)),
                  pl.BlockSpec((K, tn), lambda i, j: (0, j))],
        out_specs=[pl.BlockSpec((tm, tn), lambda i, j: (i, j)),
                   pl.BlockSpec((None, tn, tm), lambda i, j: (i // tiles_per_batch, j, i % tiles_per_batch))],
        out_shape=[jax.ShapeDtypeStruct((M, N), F32), jax.ShapeDtypeStruct((batch, N, T), BF16)],
        compiler_params=pltpu.CompilerParams(
            dimension_semantics=("parallel", "parallel"), vmem_limit_bytes=VMEM_LIMIT),
        name="matmul_t",
    )(a, b)


def _matmul_residual_kernel(a_ref, b_ref, x_ref, g_ref, o_ref):
    acc = jnp.dot(a_ref[...], b_ref[...], preferred_element_type=F32)
    o_ref[...] = x_ref[...] + g_ref[...] * acc


def _matmul_residual(a, b, x, gate):
    B, T, K = a.shape
    N = b.shape[1]
    tm, tn = _tile(T, 1024), _tile(N, 1024)
    return pl.pallas_call(
        _matmul_residual_kernel,
        grid=(B, T // tm, N // tn),
        in_specs=[pl.BlockSpec((None, tm, K), lambda bb, i, j: (bb, i, 0)),
                  pl.BlockSpec((K, tn), lambda bb, i, j: (0, j)),
                  pl.BlockSpec((None, tm, tn), lambda bb, i, j: (bb, i, j)),
                  pl.BlockSpec((None, 1, tn), lambda bb, i, j: (bb, 0, j))],
        out_specs=pl.BlockSpec((None, tm, tn), lambda bb, i, j: (bb, i, j)),
        out_shape=jax.ShapeDtypeStruct((B, T, N), F32),
        compiler_params=pltpu.CompilerParams(
            dimension_semantics=("parallel", "parallel", "parallel"), vmem_limit_bytes=VMEM_LIMIT),
        name="matmul_residual",
    )(a, b, x, gate.reshape(B, 1, N))


def _mod_kernel(c_ref, w_ref, b_ref, o_ref):
    c = c_ref[...]
    c_act = c * jax.nn.sigmoid(c)
    hi = c_act.astype(BF16)
    lo = (c_act - hi.astype(F32)).astype(BF16)
    w = w_ref[...]
    w_hi = w.astype(BF16)
    w_lo = (w - w_hi.astype(F32)).astype(BF16)
    acc = (jnp.dot(lo, w_hi, preferred_element_type=F32) + jnp.dot(hi, w_lo, preferred_element_type=F32)
           + jnp.dot(hi, w_hi, preferred_element_type=F32))
    o_ref[...] = acc + b_ref[...]


def _adaln_mod(c_pad, w_mod, b_mod):
    R, D = c_pad.shape
    N = w_mod.shape[1]
    tn = _tile(N, 512)
    return pl.pallas_call(
        _mod_kernel,
        grid=(N // tn,),
        in_specs=[pl.BlockSpec((R, D), lambda j: (0, 0)),
                  pl.BlockSpec((D, tn), lambda j: (0, j)),
                  pl.BlockSpec((1, tn), lambda j: (0, j))],
        out_specs=pl.BlockSpec((R, tn), lambda j: (0, j)),
        out_shape=jax.ShapeDtypeStruct((R, N), F32),
        compiler_params=pltpu.CompilerParams(
            dimension_semantics=("parallel",), vmem_limit_bytes=VMEM_LIMIT),
        name="adaln_mod",
    )(c_pad, w_mod, b_mod.reshape(1, N))


LOG2E = 1.4426950408889634
KEY_BLOCKS_PER_TILE = 2
PV_PAD = 16


def _pair_rms(x, lo_half, w):
    ss = x * x
    s_lo = jnp.sum(jnp.where(lo_half, ss, 0.0), axis=1, keepdims=True)
    s_hi = jnp.sum(jnp.where(lo_half, 0.0, ss), axis=1, keepdims=True)
    ms = jnp.where(lo_half, s_lo, s_hi) * (1.0 / HEAD_DIM)
    return x * lax.rsqrt(ms + NORM_EPS) * w


def _fox_attn_kernel(q_ref, k_ref, vt_ref, c_ref, z_ref, qw_ref, kw_ref, o_ref, kn_ref, vat_ref, m_ref, acc_ref,
                     s_ref, *, tq, tb):
    qi = pl.program_id(2)
    T = k_ref.shape[0]

    @pl.when(qi == 0)
    def _():
        lo_b = lax.broadcasted_iota(jnp.int32, (tb, LANES), 1) < HEAD_DIM
        row_b = lax.broadcasted_iota(jnp.int32, (LANES, tb), 0)
        top_b = row_b < HEAD_DIM

        def build(cidx, carry):
            blk = pl.ds(pl.multiple_of(cidx * tb, tb), tb)
            kn_ref[blk, :] = _pair_rms(k_ref[blk, :].astype(F32), lo_b, kw_ref[...]).astype(BF16)
            vt = vt_ref[:, blk].astype(F32)
            vat_ref[0, :, blk] = jnp.where(top_b, vt, jnp.where(row_b == HEAD_DIM, 1.0, 0.0)).astype(BF16)
            vat_ref[1, :, blk] = jnp.where(top_b, jnp.where(row_b == HEAD_DIM - 1, 1.0, 0.0), vt).astype(BF16)
            return carry

        lax.fori_loop(0, T // tb, build, 0)

    lo = lax.broadcasted_iota(jnp.int32, (tq, LANES), 1) < HEAD_DIM
    qn = _pair_rms(q_ref[...].astype(F32), lo, qw_ref[...])
    qh = (jnp.where(lo, qn, 0.0).astype(BF16), jnp.where(lo, 0.0, qn).astype(BF16))
    q_start = pl.multiple_of(qi * tq, tq)
    c_ref0 = [c_ref[h, :, pl.ds(q_start, LANES)][:, 0:1] for h in range(2)]
    tk = tq // KEY_BLOCKS_PER_TILE
    row_minus_col = (lax.broadcasted_iota(jnp.int32, (tk, tq), 0)
                     - lax.broadcasted_iota(jnp.int32, (tk, tq), 1))
    nt = (((1,), (1,)), ((), ()))
    va_rows = (slice(0, HEAD_DIM + PV_PAD), slice(HEAD_DIM - PV_PAD, LANES))

    m_ref[...] = jnp.full(m_ref.shape, NEG_BIG, F32)
    acc_ref[...] = jnp.zeros(acc_ref.shape, F32)

    def scores(start, slot):
        start = pl.multiple_of(start, tk)
        kj = kn_ref[pl.ds(start, tk), :]
        for h in range(2):
            bias = (c_ref0[h] - c_ref[h, :, pl.ds(start, tk)]) * LOG2E
            bias_col = jnp.transpose(jnp.broadcast_to(bias, (8, tk)))[:, 0:1]
            s_ref[slot, h] = lax.dot_general(kj, qh[h], nt, preferred_element_type=F32) + bias_col

    def consume(start, slot, masked):
        start = pl.multiple_of(start, tk)
        for h in range(2):
            s = s_ref[slot, h]
            if masked:
                s = jnp.where(row_minus_col <= q_start - start, s, NEG_BIG)
            m = m_ref[h]
            m_new = jnp.maximum(m, jnp.max(s, axis=0, keepdims=True))
            p = jnp.exp2((s - m_new).astype(BF16))
            acc_ref[h] = jnp.exp2(m - m_new) * acc_ref[h] + jnp.dot(
                vat_ref[h, va_rows[h], pl.ds(start, tk)], p, preferred_element_type=F32)
            m_ref[h] = m_new

    scores(0, 0)

    def body(i, carry):
        base = i * (2 * tk)
        scores(base + tk, 1)
        consume(base, 0, False)
        scores(base + 2 * tk, 0)
        consume(base + tk, 1, False)
        return carry

    lax.fori_loop(0, qi * (KEY_BLOCKS_PER_TILE // 2), body, 0)
    for t in range(KEY_BLOCKS_PER_TILE):
        if t + 1 < KEY_BLOCKS_PER_TILE:
            scores(q_start + (t + 1) * tk, (t + 1) % 2)
        consume(q_start + t * tk, t % 2, True)
    acc0, acc1 = acc_ref[0], acc_ref[1]
    o_t = jnp.concatenate([acc0[:HEAD_DIM] / acc0[HEAD_DIM:HEAD_DIM + 1, :],
                           acc1[PV_PAD:] / acc1[PV_PAD - 1:PV_PAD, :]], axis=0)
    z = z_ref[...].astype(F32)
    o_ref[...] = (jnp.transpose(o_t) * (z * jax.nn.sigmoid(z))).astype(o_ref.dtype)


def _fox_attention(qk, v_t, cum_rows, z, q_norm_w, k_norm_w):
    B, D, T = v_t.shape
    H = D // HEAD_DIM
    tq = _tile(T, 1024)
    n_pair = D // LANES
    qw = jnp.tile(q_norm_w, 2).reshape(1, LANES) * (HEAD_DIM ** -0.5 * LOG2E)
    kw = jnp.tile(k_norm_w, 2).reshape(1, LANES)
    once = pl.Buffered(1)
    return pl.pallas_call(
        functools.partial(_fox_attn_kernel, tq=tq, tb=_tile(T, 512)),
        grid=(B, H // 2, T // tq),
        in_specs=[pl.BlockSpec((None, tq, LANES), lambda b, hp, i: (b, i, hp)),
                  pl.BlockSpec((None, T, LANES), lambda b, hp, i: (b, 0, n_pair + hp), pipeline_mode=once),
                  pl.BlockSpec((None, LANES, T), lambda b, hp, i: (b, hp, 0), pipeline_mode=once),
                  pl.BlockSpec((None, 2, 1, T), lambda b, hp, i: (b, hp, 0, 0), pipeline_mode=once),
                  pl.BlockSpec((None, tq, LANES), lambda b, hp, i: (b, i, hp)),
                  pl.BlockSpec((1, LANES), lambda b, hp, i: (0, 0)),
                  pl.BlockSpec((1, LANES), lambda b, hp, i: (0, 0))],
        out_specs=pl.BlockSpec((None, tq, LANES), lambda b, hp, i: (b, i, hp)),
        out_shape=jax.ShapeDtypeStruct((B, T, D), BF16),
        scratch_shapes=[pltpu.VMEM((T, LANES), BF16), pltpu.VMEM((2, LANES, T), BF16),
                        pltpu.VMEM((2, 1, tq), F32), pltpu.VMEM((2, HEAD_DIM + PV_PAD, tq), F32),
                        pltpu.VMEM((2, 2, tq // KEY_BLOCKS_PER_TILE, tq), F32)],
        compiler_params=pltpu.CompilerParams(
            dimension_semantics=("parallel", "parallel", "arbitrary"), vmem_limit_bytes=VMEM_LIMIT),
        name="fox_attention",
    )(qk, qk, v_t, cum_rows, z, qw, kw)


def _trunc_bf16(x):
    bits = lax.bitcast_convert_type(x, jnp.uint32) & jnp.uint32(0xFFFF0000)
    return lax.bitcast_convert_type(bits, F32)


def _split3(x):
    hi = _trunc_bf16(x)
    r1 = x - hi
    mid = _trunc_bf16(r1)
    lo = r1 - mid
    return hi.astype(BF16), mid.astype(BF16), lo.astype(BF16)


_NN = (((1,), (0,)), ((), ()))
_NT = (((1,), (1,)), ((), ()))
_TN = (((0,), (0,)), ((), ()))


def _hl(x):
    hi = x.astype(BF16)
    return hi, (x - hi.astype(F32)).astype(BF16)


def _h(x):
    return (x.astype(BF16),)


def _mm3(a, b, dims=_NN):
    dg = functools.partial(lax.dot_general, dimension_numbers=dims, preferred_element_type=F32)
    if len(a) == 1 or len(b) == 1:
        return dg(a[0], b[0])
    return (dg(a[1], b[0]) + dg(a[0], b[1])) + dg(a[0], b[0])


def _rwkv_chunks(groups, head_masks):
    C, W = groups[0][0].shape
    assert C == 2 ** (INV_SQUARINGS + 1) == HEAD_DIM
    n = range(len(groups))
    t_idx = lax.broadcasted_iota(jnp.int32, (C, W), 0)
    i_idx = lax.broadcasted_iota(jnp.int32, (C, W), 1) & (HEAD_DIM - 1)
    strict = i_idx < t_idx
    incl = i_idx <= t_idx
    eye = jnp.where(i_idx == t_idx, 1.0, 0.0)
    rw = lax.broadcasted_iota(jnp.int32, (W, W), 0) >> 6
    cw = lax.broadcasted_iota(jnp.int32, (W, W), 1) >> 6
    same_head = rw == cw

    def bd(parts):
        return tuple(jnp.concatenate([jnp.where(mk, p, jnp.zeros_like(p)) for mk in head_masks], axis=0)
                     for p in parts)

    rr = lax.broadcasted_iota(jnp.int32, (C, C), 0)
    cc = lax.broadcasted_iota(jnp.int32, (C, C), 1)
    tri = jnp.where(cc <= rr, 1.0, 0.0).astype(BF16)
    mm = functools.partial(jnp.dot, preferred_element_type=F32)

    r, lw, k, v, a, b, S = [[grp[i] for grp in groups] for i in range(7)]
    lw3 = [_split3(lw[g]) for g in n]
    cum = [(mm(tri, lw3[g][2]) + mm(tri, lw3[g][1])) + mm(tri, lw3[g][0]) for g in n]
    e_neg = [jnp.exp(-cum[g]) for g in n]
    AR = [_h(jnp.concatenate([a[g] * jnp.exp(cum[g] - lw[g]), r[g] * jnp.exp(cum[g])], axis=0))
          for g in n]
    sB = [_mm3(AR[g], bd(_h(b[g] * e_neg[g])), _NT) for g in n]
    sK = [_mm3(AR[g], bd(_h(k[g] * e_neg[g])), _NT) for g in n]
    A_ab = [jnp.where(strict, sB[g][:C], 0.0) for g in n]

    P = [eye + A_ab[g] for g in n]
    Apow = [_hl(A_ab[g]) for g in n]
    for _ in range(INV_SQUARINGS):
        Apow = [_hl(_mm3(Apow[g], bd(Apow[g]))) for g in n]
        P = [P[g] + _mm3(Apow[g], bd(_hl(P[g]))) for g in n]

    A_k = [_h(jnp.concatenate([jnp.where(strict, sK[g][:C], 0.0), jnp.where(incl, sK[g][C:], 0.0)], axis=0))
           for g in n]
    XY = [_mm3(AR[g], _h(S[g]), _NT) + _mm3(A_k[g], bd(_h(v[g]))) for g in n]
    U = [_mm3(_hl(P[g]), bd(_hl(XY[g][:C]))) for g in n]
    Y = [XY[g][C:] + _mm3(_h(jnp.where(incl, sB[g][C:], 0.0)), bd(_h(U[g]))) for g in n]
    out = []
    for g in n:
        cum_end = cum[g][C - 1:C, :]
        e_end = jnp.exp(cum_end - cum[g])
        upd = _mm3(_h(jnp.concatenate([U[g], v[g]], axis=0)),
                   _h(jnp.concatenate([b[g] * e_end, k[g] * e_end], axis=0)), _TN)
        out.append((Y[g], S[g] * jnp.exp(cum_end) + jnp.where(same_head, upd, 0.0)))
    return out


_RWKV_ROW_INPUTS = 8
_RWKV_VEC_PARAMS = 8


def _rwkv_kernel(*refs):
    row_refs = refs[:_RWKV_ROW_INPUTS]
    vec_refs = refs[_RWKV_ROW_INPUTS:_RWKV_ROW_INPUTS + _RWKV_VEC_PARAMS]
    o_ref, s_ref = refs[_RWKV_ROW_INPUTS + _RWKV_VEC_PARAMS:]

    @pl.when(pl.program_id(2) == 0)
    def _():
        s_ref[...] = jnp.zeros_like(s_ref)

    C, W = row_refs[0].shape[0], MXU_DIM
    lane_head = lax.broadcasted_iota(jnp.int32, (C, W), 1) >> 6
    in_head = [lane_head == h for h in range(HEADS_PER_GROUP)]
    col_slices = [slice(g * W, (g + 1) * W) for g in range(row_refs[0].shape[1] // W)]
    n = range(len(col_slices))

    def head_sum(x):
        parts = [jnp.sum(jnp.where(m, x, 0.0), axis=1, keepdims=True) for m in in_head]
        out = parts[-1]
        for m, p in zip(in_head[-2::-1], parts[-2::-1]):
            out = jnp.where(m, p, out)
        return out

    rows = [[ref[:, cols].astype(F32) for ref in row_refs] for cols in col_slices]
    vecs = [[ref[:, cols] for ref in vec_refs] for cols in col_slices]
    groups, keep = [], []
    for g in n:
        r, k, v, wl, al, vl, vf, z = rows[g]
        w0, a0, v0, k_k, k_a, r_k, _, _ = vecs[g]
        t = -(w0 + wl)
        softplus = jnp.maximum(t, 0.0) + jnp.log(1.0 + jnp.exp(-jnp.abs(t)))
        lw = -jnp.exp(-softplus - 0.5)
        rate = jax.nn.sigmoid(a0 + al)
        vv = v + (vf - v) * jax.nn.sigmoid(v0 + vl)
        kk = k * k_k
        kk = kk / jnp.maximum(jnp.sqrt(head_sum(kk * kk)), 1e-12)
        k2 = k * (1.0 + (rate - 1.0) * k_a)
        groups.append((r, lw, k2, vv, -kk, kk * rate, s_ref[g]))
        keep.append((head_sum(r * k2 * r_k) * vv, z))
    results = _rwkv_chunks(groups, in_head)
    for g, cols in enumerate(col_slices):
        y, s_new = results[g]
        bonus, z = keep[g]
        lnx_w, lnx_b = vecs[g][6], vecs[g][7]
        d = y - head_sum(y) * (1.0 / HEAD_DIM)
        var = head_sum(d * d) * (1.0 / HEAD_DIM)
        yn = d * lax.rsqrt(var + LNX_EPS) * lnx_w + lnx_b
        o_ref[:, cols] = ((yn + bonus) * (z * jax.nn.sigmoid(z))).astype(o_ref.dtype)
        s_ref[g] = s_new


def _rwkv_mix(r, k, v, wl, al, vl, v_first, z, vec_params):
    B, T, D = r.shape
    C, W = RWKV_CHUNK, MXU_DIM
    groups = next(g for g in (8, 4, 2, 1) if D % (g * W) == 0)
    spec = pl.BlockSpec((None, C, groups * W), lambda bb, g, c: (bb, c, g))
    vec_spec = pl.BlockSpec((1, groups * W), lambda bb, g, c: (0, g))
    return pl.pallas_call(
        _rwkv_kernel,
        grid=(B, D // (groups * W), T // C),
        in_specs=[spec] * _RWKV_ROW_INPUTS + [vec_spec] * _RWKV_VEC_PARAMS,
        out_specs=spec,
        out_shape=jax.ShapeDtypeStruct((B, T, D), BF16),
        scratch_shapes=[pltpu.VMEM((groups, W, W), F32)],
        compiler_params=pltpu.CompilerParams(
            dimension_semantics=("parallel", "parallel", "arbitrary"), vmem_limit_bytes=VMEM_LIMIT),
        name="rwkv7_mix",
    )(r, k, v, wl, al, vl, v_first, z, *[p.reshape(1, D) for p in vec_params])


def _pad_cols(w, n):
    return jnp.pad(w, ((0, 0), (0, n - w.shape[1])))


def _pad_rows(w, n):
    return jnp.pad(w, ((0, n - w.shape[0]), (0, 0)))


def _fox_layer(x, h, gate, w_in, b_f, q_norm_w, k_norm_w, w_out):
    B, T, D = x.shape
    H = D // HEAD_DIM
    hb = h.reshape(B * T, D)
    w_b = w_in.astype(BF16)
    qk = _matmul(hb, w_b[:, :2 * D], BF16).reshape(B, T, 2 * D)
    v, v_t = _matmul_and_transposed(hb, w_b[:, 2 * D:3 * D], B)
    v = v.reshape(B, T, D)
    z = _matmul(hb, w_b[:, 3 * D:4 * D], BF16).reshape(B, T, D)
    f_logit = _matmul(hb, _pad_cols(w_b[:, 4 * D:], LANES), F32)[:, :H].reshape(B, T, H) + b_f
    cum = jnp.cumsum(jax.nn.log_sigmoid(f_logit), axis=1)
    cum_rows = jnp.transpose(cum, (0, 2, 1)).reshape(B, H, 1, T)
    g = _fox_attention(qk, v_t, cum_rows, z, q_norm_w, k_norm_w)
    return _matmul_residual(g, w_out.astype(BF16), x, gate), v


def _lora(xb, w1, w2, act):
    rank = w1.shape[1]
    rp = -(-rank // LANES) * LANES
    mid = act(_matmul(xb, _pad_cols(w1, rp).astype(BF16)))
    return _matmul(mid.astype(BF16), _pad_rows(w2, rp).astype(BF16))


def _rwkv_layer(x, mixes, gate, v_first, w_in, w0, w1, w2, a0, a1, a2, v0, v1, v2,
                k_k, k_a, r_k, lnx_w, lnx_b, w_out):
    B, T, D = x.shape
    M = B * T
    shp = (B, T, D)
    xr, xw, xk, xv, xa, xg = [t.reshape(M, D) for t in mixes]
    r = _matmul(xr, w_in[0].astype(BF16)).reshape(shp)
    k = _matmul(xk, w_in[1].astype(BF16)).reshape(shp)
    v = _matmul(xv, w_in[2].astype(BF16)).reshape(shp)
    z = _matmul(xg, w_in[3].astype(BF16), BF16).reshape(shp)
    wl = _lora(xw, w1, w2, jnp.tanh).reshape(shp)
    al = _lora(xa, a1, a2, lambda t: t).reshape(shp)
    vl = _lora(xv, v1, v2, lambda t: t).reshape(shp)
    g = _rwkv_mix(r, k, v, wl, al, vl, v_first, z,
                  (w0, a0, v0, k_k, k_a, r_k.reshape(D), lnx_w, lnx_b))
    return _matmul_residual(g, w_out.astype(BF16), x, gate)


def kernel(x, c, norm_w, w_mod, b_mod, fox_w_in, fox_b_f, fox_q_norm_w, fox_k_norm_w, fox_w_out, rwkv_mu, rwkv_w_in, rwkv_w0, rwkv_w1, rwkv_w2, rwkv_a0, rwkv_a1, rwkv_a2, rwkv_v0, rwkv_v1, rwkv_v2, rwkv_k_k, rwkv_k_a, rwkv_r_k, rwkv_lnx_w, rwkv_lnx_b, rwkv_w_out):
    B, T, D = x.shape
    depth = norm_w.shape[0]
    c_pad = jnp.pad(c, ((0, (-B) % 8), (0, 0)))
    v_first = None
    for i in range(depth):
        mod = _adaln_mod(c_pad, w_mod[i], b_mod[i])[:B]
        shift, scale, gate = mod[:, :D], mod[:, D:2 * D], mod[:, 2 * D:]
        j = i // 2
        if i % 2 == 0:
            h = _norm_mod(x, norm_w[i], scale, shift)
            x, v = _fox_layer(x, h, gate, fox_w_in[j], fox_b_f[j], fox_q_norm_w[j], fox_k_norm_w[j],
                              fox_w_out[j])
            if v_first is None:
                v_first = v
        else:
            mixes = _norm_mod_mix(x, norm_w[i], scale, shift, rwkv_mu[j])
            x = _rwkv_layer(x, mixes, gate, v_first, rwkv_w_in[j], rwkv_w0[j], rwkv_w1[j],
                            rwkv_w2[j], rwkv_a0[j], rwkv_a1[j], rwkv_a2[j], rwkv_v0[j], rwkv_v1[j],
                            rwkv_v2[j], rwkv_k_k[j], rwkv_k_a[j], rwkv_r_k[j], rwkv_lnx_w[j],
                            rwkv_lnx_b[j], rwkv_w_out[j])
    return x
```

```python
import functools

import jax
import jax.numpy as jnp
from jax import lax
from jax.experimental import pallas as pl
from jax.experimental.pallas import tpu as pltpu

F32 = jnp.float32
BF16 = jnp.bfloat16

HEAD_DIM = 64
LANES = 128
MXU_DIM = 256
RWKV_CHUNK = 64
INV_SQUARINGS = 5
HEADS_PER_GROUP = MXU_DIM // HEAD_DIM
NORM_EPS = 1e-6
LNX_EPS = 64e-5
VMEM_LIMIT = 56 * 1024 * 1024
NEG_BIG = -1e30


def _tile(n, pref):
    t = min(n, pref)
    while n % t:
        t //= 2
    return t


def _norm_mod_kernel(x_ref, nw_ref, sc_ref, sh_ref, o_ref):
    x = x_ref[...]
    ms = jnp.mean(x * x, axis=-1, keepdims=True)
    y = x * lax.rsqrt(ms + NORM_EPS) * nw_ref[...]
    o_ref[...] = (y * (1.0 + sc_ref[...]) + sh_ref[...]).astype(o_ref.dtype)


def _norm_mod(x, norm_w, scale, shift):
    B, T, D = x.shape
    tm = _tile(T, 512)
    row = pl.BlockSpec((None, tm, D), lambda b, i: (b, i, 0))
    per_b = pl.BlockSpec((None, 1, D), lambda b, i: (b, 0, 0))
    return pl.pallas_call(
        _norm_mod_kernel,
        grid=(B, T // tm),
        in_specs=[row, pl.BlockSpec((1, D), lambda b, i: (0, 0)), per_b, per_b],
        out_specs=row,
        out_shape=jax.ShapeDtypeStruct((B, T, D), BF16),
        compiler_params=pltpu.CompilerParams(
            dimension_semantics=("parallel", "parallel"), vmem_limit_bytes=VMEM_LIMIT),
        name="norm_mod",
    )(x, norm_w.reshape(1, D), scale.reshape(B, 1, D), shift.reshape(B, 1, D))


def _norm_mod_mix_kernel(x_ref, xp_ref, nw_ref, sc_ref, sh_ref, mu_ref, *o_refs):
    def modulated(x):
        ms = jnp.mean(x * x, axis=-1, keepdims=True)
        return x * lax.rsqrt(ms + NORM_EPS) * nw_ref[...] * (1.0 + sc_ref[...]) + sh_ref[...]

    h = modulated(x_ref[...])
    above = modulated(xp_ref[...])[7:8, :]
    above = jnp.where(pl.program_id(1) == 0, 0.0, above)
    row = lax.broadcasted_iota(jnp.int32, h.shape, 0)
    xx = jnp.where(row == 0, above, pltpu.roll(h, 1, axis=0)) - h
    for i, o_ref in enumerate(o_refs[:-2]):
        o_ref[...] = (h + xx * mu_ref[i:i + 1, :]).astype(o_ref.dtype)
    o_refs[-2][...] = h.astype(o_refs[-2].dtype)
    o_refs[-1][...] = xx.astype(o_refs[-1].dtype)


def _norm_mod_mix(x, norm_w, scale, shift, mu):
    B, T, D = x.shape
    n = mu.shape[0] + 2
    tm = _tile(T, 512)
    row = pl.BlockSpec((None, tm, D), lambda b, i: (b, i, 0))
    prev = pl.BlockSpec((None, 8, D), lambda b, i: (b, jnp.maximum(i * (tm // 8) - 1, 0), 0))
    per_b = pl.BlockSpec((None, 1, D), lambda b, i: (b, 0, 0))
    return pl.pallas_call(
        _norm_mod_mix_kernel,
        grid=(B, T // tm),
        in_specs=[row, prev, pl.BlockSpec((1, D), lambda b, i: (0, 0)), per_b, per_b,
                  pl.BlockSpec((n - 2, D), lambda b, i: (0, 0))],
        out_specs=[row] * n,
        out_shape=[jax.ShapeDtypeStruct((B, T, D), BF16)] * n,
        compiler_params=pltpu.CompilerParams(
            dimension_semantics=("parallel", "parallel"), vmem_limit_bytes=VMEM_LIMIT),
        name="norm_mod_mix",
    )(x, x, norm_w.reshape(1, D), scale.reshape(B, 1, D), shift.reshape(B, 1, D), mu)


def _matmul_kernel(a_ref, b_ref, o_ref):
    o_ref[...] = jnp.dot(a_ref[...], b_ref[...], preferred_element_type=F32).astype(o_ref.dtype)


def _matmul(a, b, out_dtype=F32):
    M, K = a.shape
    N = b.shape[1]
    tm, tn = _tile(M, 1024), _tile(N, 1024)
    return pl.pallas_call(
        _matmul_kernel,
        grid=(M // tm, N // tn),
        in_specs=[pl.BlockSpec((tm, K), lambda i, j: (i, 0)),
                  pl.BlockSpec((K, tn), lambda i, j: (0, j))],
        out_specs=pl.BlockSpec((tm, tn), lambda i, j: (i, j)),
        out_shape=jax.ShapeDtypeStruct((M, N), out_dtype),
        compiler_params=pltpu.CompilerParams(
            dimension_semantics=("parallel", "parallel"), vmem_limit_bytes=VMEM_LIMIT),
        name="matmul",
    )(a, b)


def _matmul_t_kernel(a_ref, b_ref, o_ref, ot_ref):
    acc = jnp.dot(a_ref[...], b_ref[...], preferred_element_type=F32)
    o_ref[...] = acc
    ot_ref[...] = jnp.transpose(acc).astype(ot_ref.dtype)


def _matmul_and_transposed(a, b, batch):
    M, K = a.shape
    N = b.shape[1]
    T = M // batch
    tm, tn = _tile(T, 1024), _tile(N, 1024)
    tiles_per_batch = T // tm
    return pl.pallas_call(
        _matmul_t_kernel,
        grid=(M // tm, N // tn),
        in_specs=[pl.BlockSpec((tm, K), lambda i, j: (i, 0)),
                  pl.BlockSpec((K, tn), lambda i, j: (0, j))],
        out_specs=[pl.BlockSpec((tm, tn), lambda i, j: (i, j)),
                   pl.BlockSpec((None, tn, tm), lambda i, j: (i // tiles_per_batch, j, i % tiles_per_batch))],
        out_shape=[jax.ShapeDtypeStruct((M, N), F32), jax.ShapeDtypeStruct((batch, N, T), BF16)],
        compiler_params=pltpu.CompilerParams(
            dimension_semantics=("parallel", "parallel"), vmem_limit_bytes=VMEM_LIMIT),
        name="matmul_t",
    )(a, b)


def _lora_in_kernel(h_ref, xx_ref, wh_ref, wx_ref, o_ref, *, n_tanh):
    acc = (jnp.dot(h_ref[...], wh_ref[...], preferred_element_type=F32)
           + jnp.dot(xx_ref[...], wx_ref[...], preferred_element_type=F32))
    col = lax.broadcasted_iota(jnp.int32, acc.shape, 1)
    o_ref[...] = jnp.where(col < n_tanh, jnp.tanh(acc), acc).astype(o_ref.dtype)


def _lora_in(h, xx, w_h, w_x, n_tanh):
    M, K = h.shape
    R = w_h.shape[1]
    tm = _tile(M, 1024)
    lhs = pl.BlockSpec((tm, K), lambda i: (i, 0))
    rhs = pl.BlockSpec((K, R), lambda i: (0, 0))
    return pl.pallas_call(
        functools.partial(_lora_in_kernel, n_tanh=n_tanh),
        grid=(M // tm,),
        in_specs=[lhs, lhs, rhs, rhs],
        out_specs=pl.BlockSpec((tm, R), lambda i: (i, 0)),
        out_shape=jax.ShapeDtypeStruct((M, R), BF16),
        compiler_params=pltpu.CompilerParams(
            dimension_semantics=("parallel",), vmem_limit_bytes=VMEM_LIMIT),
        name="lora_in",
    )(h, xx, w_h, w_x)


def _matmul_residual_kernel(a_ref, b_ref, x_ref, g_ref, o_ref):
    acc = jnp.dot(a_ref[...], b_ref[...], preferred_element_type=F32)
    o_ref[...] = x_ref[...] + g_ref[...] * acc


def _matmul_residual(a, b, x, gate):
    B, T, K = a.shape
    N = b.shape[1]
    tm, tn = _tile(T, 1024), _tile(N, 1024)
    return pl.pallas_call(
        _matmul_residual_kernel,
        grid=(B, T // tm, N // tn),
        in_specs=[pl.BlockSpec((None, tm, K), lambda bb, i, j: (bb, i, 0)),
                  pl.BlockSpec((K, tn), lambda bb, i, j: (0, j)),
                  pl.BlockSpec((None, tm, tn), lambda bb, i, j: (bb, i, j)),
                  pl.BlockSpec((None, 1, tn), lambda bb, i, j: (bb, 0, j))],
        out_specs=pl.BlockSpec((None, tm, tn), lambda bb, i, j: (bb, i, j)),
        out_shape=jax.ShapeDtypeStruct((B, T, N), F32),
        compiler_params=pltpu.CompilerParams(
            dimension_semantics=("parallel", "parallel", "parallel"), vmem_limit_bytes=VMEM_LIMIT),
        name="matmul_residual",
    )(a, b, x, gate.reshape(B, 1, N))


def _mod_kernel(c_ref, w_ref, b_ref, o_ref):
    c = c_ref[...]
    c_act = c * jax.nn.sigmoid(c)
    hi = c_act.astype(BF16)
    lo = (c_act - hi.astype(F32)).astype(BF16)
    w = w_ref[...]
    w_hi = w.astype(BF16)
    w_lo = (w - w_hi.astype(F32)).astype(BF16)
    acc = (jnp.dot(lo, w_hi, preferred_element_type=F32) + jnp.dot(hi, w_lo, preferred_element_type=F32)
           + jnp.dot(hi, w_hi, preferred_element_type=F32))
    o_ref[...] = acc + b_ref[...]


def _adaln_mod(c_pad, w_mod, b_mod):
    R, D = c_pad.shape
    N = w_mod.shape[1]
    tn = _tile(N, 512)
    return pl.pallas_call(
        _mod_kernel,
        grid=(N // tn,),
        in_specs=[pl.BlockSpec((R, D), lambda j: (0, 0)),
                  pl.BlockSpec((D, tn), lambda j: (0, j)),
                  pl.BlockSpec((1, tn), lambda j: (0, j))],
        out_specs=pl.BlockSpec((R, tn), lambda j: (0, j)),
        out_shape=jax.ShapeDtypeStruct((R, N), F32),
        compiler_params=pltpu.CompilerParams(
            dimension_semantics=("parallel",), vmem_limit_bytes=VMEM_LIMIT),
        name="adaln_mod",
    )(c_pad, w_mod, b_mod.reshape(1, N))


LOG2E = 1.4426950408889634
KEY_BLOCKS_PER_TILE = 2
PV_PAD = 16


def _pair_rms(x, lo_half, w):
    ss = x * x
    s_lo = jnp.sum(jnp.where(lo_half, ss, 0.0), axis=1, keepdims=True)
    s_hi = jnp.sum(jnp.where(lo_half, 0.0, ss), axis=1, keepdims=True)
    ms = jnp.where(lo_half, s_lo, s_hi) * (1.0 / HEAD_DIM)
    return x * lax.rsqrt(ms + NORM_EPS) * w


def _fox_attn_kernel(q_ref, k_ref, vt_ref, c_ref, z_ref, qw_ref, kw_ref, o_ref, kn_ref, vat_ref, m_ref, acc_ref,
                     s_ref, *, tq, tb):
    qi = pl.program_id(2)
    T = k_ref.shape[0]

    @pl.when(qi == 0)
    def _():
        lo_b = lax.broadcasted_iota(jnp.int32, (tb, LANES), 1) < HEAD_DIM
        row_b = lax.broadcasted_iota(jnp.int32, (LANES, tb), 0)
        top_b = row_b < HEAD_DIM

        def build(cidx, carry):
            blk = pl.ds(pl.multiple_of(cidx * tb, tb), tb)
            kn_ref[blk, :] = _pair_rms(k_ref[blk, :].astype(F32), lo_b, kw_ref[...]).astype(BF16)
            vt = vt_ref[:, blk].astype(F32)
            vat_ref[0, :, blk] = jnp.where(top_b, vt, jnp.where(row_b == HEAD_DIM, 1.0, 0.0)).astype(BF16)
            vat_ref[1, :, blk] = jnp.where(top_b, jnp.where(row_b == HEAD_DIM - 1, 1.0, 0.0), vt).astype(BF16)
            return carry

        lax.fori_loop(0, T // tb, build, 0)

    lo = lax.broadcasted_iota(jnp.int32, (tq, LANES), 1) < HEAD_DIM
    qn = _pair_rms(q_ref[...].astype(F32), lo, qw_ref[...])
    qh = (jnp.where(lo, qn, 0.0).astype(BF16), jnp.where(lo, 0.0, qn).astype(BF16))
    q_start = pl.multiple_of(qi * tq, tq)
    c_ref0 = [c_ref[h, :, pl.ds(q_start, LANES)][:, 0:1] for h in range(2)]
    tk = tq // KEY_BLOCKS_PER_TILE
    row_minus_col = (lax.broadcasted_iota(jnp.int32, (tk, tq), 0)
                     - lax.broadcasted_iota(jnp.int32, (tk, tq), 1))
    nt = (((1,), (1,)), ((), ()))
    va_rows = (slice(0, HEAD_DIM + PV_PAD), slice(HEAD_DIM - PV_PAD, LANES))

    m_ref[...] = jnp.full(m_ref.shape, NEG_BIG, F32)
    acc_ref[...] = jnp.zeros(acc_ref.shape, F32)

    def scores(start, slot):
        start = pl.multiple_of(start, tk)
        kj = kn_ref[pl.ds(start, tk), :]
        for h in range(2):
            bias = (c_ref0[h] - c_ref[h, :, pl.ds(start, tk)]) * LOG2E
            bias_col = jnp.transpose(jnp.broadcast_to(bias, (8, tk)))[:, 0:1]
            s_ref[slot, h] = lax.dot_general(kj, qh[h], nt, preferred_element_type=F32) + bias_col

    def consume(start, slot, masked):
        start = pl.multiple_of(start, tk)
        for h in range(2):
            s = s_ref[slot, h]
            if masked:
                s = jnp.where(row_minus_col <= q_start - start, s, NEG_BIG)
            m = m_ref[h]
            m_new = jnp.maximum(m, jnp.max(s, axis=0, keepdims=True))
            p = jnp.exp2((s - m_new).astype(BF16))
            acc_ref[h] = jnp.exp2(m - m_new) * acc_ref[h] + jnp.dot(
                vat_ref[h, va_rows[h], pl.ds(start, tk)], p, preferred_element_type=F32)
            m_ref[h] = m_new

    scores(0, 0)

    def body(i, carry):
        base = i * (2 * tk)
        scores(base + tk, 1)
        consume(base, 0, False)
        scores(base + 2 * tk, 0)
        consume(base + tk, 1, False)
        return carry

    lax.fori_loop(0, qi * (KEY_BLOCKS_PER_TILE // 2), body, 0)
    for t in range(KEY_BLOCKS_PER_TILE):
        if t + 1 < KEY_BLOCKS_PER_TILE:
            scores(q_start + (t + 1) * tk, (t + 1) % 2)
        consume(q_start + t * tk, t % 2, True)
    acc0, acc1 = acc_ref[0], acc_ref[1]
    o_t = jnp.concatenate([acc0[:HEAD_DIM] / acc0[HEAD_DIM:HEAD_DIM + 1, :],
                           acc1[PV_PAD:] / acc1[PV_PAD - 1:PV_PAD, :]], axis=0)
    z = z_ref[...].astype(F32)
    o_ref[...] = (jnp.transpose(o_t) * (z * jax.nn.sigmoid(z))).astype(o_ref.dtype)


def _fox_attention(qk, v_t, cum_rows, z, q_norm_w, k_norm_w):
    B, D, T = v_t.shape
    H = D // HEAD_DIM
    tq = _tile(T, 1024)
    n_pair = D // LANES
    qw = jnp.tile(q_norm_w, 2).reshape(1, LANES) * (HEAD_DIM ** -0.5 * LOG2E)
    kw = jnp.tile(k_norm_w, 2).reshape(1, LANES)
    once = pl.Buffered(1)
    return pl.pallas_call(
        functools.partial(_fox_attn_kernel, tq=tq, tb=_tile(T, 512)),
        grid=(B, H // 2, T // tq),
        in_specs=[pl.BlockSpec((None, tq, LANES), lambda b, hp, i: (b, i, hp)),
                  pl.BlockSpec((None, T, LANES), lambda b, hp, i: (b, 0, n_pair + hp), pipeline_mode=once),
                  pl.BlockSpec((None, LANES, T), lambda b, hp, i: (b, hp, 0), pipeline_mode=once),
                  pl.BlockSpec((None, 2, 1, T), lambda b, hp, i: (b, hp, 0, 0), pipeline_mode=once),
                  pl.BlockSpec((None, tq, LANES), lambda b, hp, i: (b, i, hp)),
                  pl.BlockSpec((1, LANES), lambda b, hp, i: (0, 0)),
                  pl.BlockSpec((1, LANES), lambda b, hp, i: (0, 0))],
        out_specs=pl.BlockSpec((None, tq, LANES), lambda b, hp, i: (b, i, hp)),
        out_shape=jax.ShapeDtypeStruct((B, T, D), BF16),
        scratch_shapes=[pltpu.VMEM((T, LANES), BF16), pltpu.VMEM((2, LANES, T), BF16),
                        pltpu.VMEM((2, 1, tq), F32), pltpu.VMEM((2, HEAD_DIM + PV_PAD, tq), F32),
                        pltpu.VMEM((2, 2, tq // KEY_BLOCKS_PER_TILE, tq), F32)],
        compiler_params=pltpu.CompilerParams(
            dimension_semantics=("parallel", "parallel", "arbitrary"), vmem_limit_bytes=VMEM_LIMIT),
        name="fox_attention",
    )(qk, qk, v_t, cum_rows, z, qw, kw)


def _trunc_bf16(x):
    bits = lax.bitcast_convert_type(x, jnp.uint32) & jnp.uint32(0xFFFF0000)
    return lax.bitcast_convert_type(bits, F32)


def _split3(x):
    hi = _trunc_bf16(x)
    r1 = x - hi
    mid = _trunc_bf16(r1)
    lo = r1 - mid
    return hi.astype(BF16), mid.astype(BF16), lo.astype(BF16)


_NN = (((1,), (0,)), ((), ()))
_NT = (((1,), (1,)), ((), ()))
_TN = (((0,), (0,)), ((), ()))


def _hl(x):
    hi = x.astype(BF16)
    return hi, (x - hi.astype(F32)).astype(BF16)


def _h(x):
    return (x.astype(BF16),)


def _mm3(a, b, dims=_NN):
    dg = functools.partial(lax.dot_general, dimension_numbers=dims, preferred_element_type=F32)
    if len(a) == 1 or len(b) == 1:
        return dg(a[0], b[0])
    return (dg(a[1], b[0]) + dg(a[0], b[1])) + dg(a[0], b[0])


def _rwkv_chunks(groups, head_masks):
    C, W = groups[0][0].shape
    assert C == 2 ** (INV_SQUARINGS + 1) == HEAD_DIM
    n = range(len(groups))
    t_idx = lax.broadcasted_iota(jnp.int32, (C, W), 0)
    i_idx = lax.broadcasted_iota(jnp.int32, (C, W), 1) & (HEAD_DIM - 1)
    strict = i_idx < t_idx
    incl = i_idx <= t_idx
    eye = jnp.where(i_idx == t_idx, 1.0, 0.0)
    rw = lax.broadcasted_iota(jnp.int32, (W, W), 0) >> 6
    cw = lax.broadcasted_iota(jnp.int32, (W, W), 1) >> 6
    same_head = rw == cw

    def bd(parts):
        return tuple(jnp.concatenate([jnp.where(mk, p, jnp.zeros_like(p)) for mk in head_masks], axis=0)
                     for p in parts)

    rr = lax.broadcasted_iota(jnp.int32, (C, C), 0)
    cc = lax.broadcasted_iota(jnp.int32, (C, C), 1)
    tri = jnp.where(cc <= rr, 1.0, 0.0).astype(BF16)
    mm = functools.partial(jnp.dot, preferred_element_type=F32)

    r, lw, k, v, a, b, S = [[grp[i] for grp in groups] for i in range(7)]
    lw3 = [_split3(lw[g]) for g in n]
    cum = [(mm(tri, lw3[g][2]) + mm(tri, lw3[g][1])) + mm(tri, lw3[g][0]) for g in n]
    e_neg = [jnp.exp(-cum[g]) for g in n]
    AR = [_h(jnp.concatenate([a[g] * jnp.exp(cum[g] - lw[g]), r[g] * jnp.exp(cum[g])], axis=0))
          for g in n]
    sB = [_mm3(AR[g], bd(_h(b[g] * e_neg[g])), _NT) for g in n]
    sK = [_mm3(AR[g], bd(_h(k[g] * e_neg[g])), _NT) for g in n]
    A_ab = [jnp.where(strict, sB[g][:C], 0.0) for g in n]

    P = [eye + A_ab[g] for g in n]
    Apow = [_hl(A_ab[g]) for g in n]
    Apow = [_hl(_mm3(Apow[g], bd(Apow[g]))) for g in n]
    for step in range(INV_SQUARINGS):
        last = step == INV_SQUARINGS - 1
        lhs = [_hl(P[g]) if last else tuple(jnp.concatenate([p_part, a_part], axis=0)
                                            for p_part, a_part in zip(_hl(P[g]), Apow[g])) for g in n]
        prod = [_mm3(lhs[g], bd(Apow[g])) for g in n]
        P = [P[g] + prod[g][:C] for g in n]
        if not last:
            Apow = [_hl(prod[g][C:]) for g in n]

    A_k = [_h(jnp.concatenate([jnp.where(strict, sK[g][:C], 0.0), jnp.where(incl, sK[g][C:], 0.0)], axis=0))
           for g in n]
    XY = [_mm3(AR[g], _h(S[g]), _NT) + _mm3(A_k[g], bd(_h(v[g]))) for g in n]
    U = [_mm3(_hl(P[g]), bd(_hl(XY[g][:C]))) for g in n]
    Y = [XY[g][C:] + _mm3(_h(jnp.where(incl, sB[g][C:], 0.0)), bd(_h(U[g]))) for g in n]
    out = []
    for g in n:
        cum_end = cum[g][C - 1:C, :]
        e_end = jnp.exp(cum_end - cum[g])
        upd = _mm3(_h(jnp.concatenate([U[g], v[g]], axis=0)),
                   _h(jnp.concatenate([b[g] * e_end, k[g] * e_end], axis=0)), _TN)
        out.append((Y[g], S[g] * jnp.exp(cum_end) + jnp.where(same_head, upd, 0.0)))
    return out


_RWKV_ROW_INPUTS = 8
_RWKV_VEC_PARAMS = 8


def _rwkv_kernel(*refs):
    row_refs = refs[:_RWKV_ROW_INPUTS]
    vec_refs = refs[_RWKV_ROW_INPUTS:_RWKV_ROW_INPUTS + _RWKV_VEC_PARAMS]
    o_ref, s_ref = refs[_RWKV_ROW_INPUTS + _RWKV_VEC_PARAMS:]

    @pl.when(pl.program_id(2) == 0)
    def _():
        s_ref[...] = jnp.zeros_like(s_ref)

    C, W = row_refs[0].shape[0], MXU_DIM
    lane_head = lax.broadcasted_iota(jnp.int32, (C, W), 1) >> 6
    in_head = [lane_head == h for h in range(HEADS_PER_GROUP)]
    col_slices = [slice(g * W, (g + 1) * W) for g in range(row_refs[0].shape[1] // W)]
    n = range(len(col_slices))

    def head_sum(x):
        parts = [jnp.sum(jnp.where(m, x, 0.0), axis=1, keepdims=True) for m in in_head]
        out = parts[-1]
        for m, p in zip(in_head[-2::-1], parts[-2::-1]):
            out = jnp.where(m, p, out)
        return out

    rows = [[ref[:, cols].astype(F32) for ref in row_refs] for cols in col_slices]
    vecs = [[ref[:, cols] for ref in vec_refs] for cols in col_slices]
    groups, keep = [], []
    for g in n:
        r, k, v, wl, al, vl, vf, z = rows[g]
        w0, a0, v0, k_k, k_a, r_k, _, _ = vecs[g]
        t = -(w0 + wl)
        softplus = jnp.maximum(t, 0.0) + jnp.log(1.0 + jnp.exp(-jnp.abs(t)))
        lw = -jnp.exp(-softplus - 0.5)
        rate = jax.nn.sigmoid(a0 + al)
        vv = v + (vf - v) * jax.nn.sigmoid(v0 + vl)
        kk = k * k_k
        kk = kk / jnp.maximum(jnp.sqrt(head_sum(kk * kk)), 1e-12)
        k2 = k * (1.0 + (rate - 1.0) * k_a)
        groups.append((r, lw, k2, vv, -kk, kk * rate, s_ref[g]))
        keep.append((head_sum(r * k2 * r_k) * vv, z))
    results = _rwkv_chunks(groups, in_head)
    for g, cols in enumerate(col_slices):
        y, s_new = results[g]
        bonus, z = keep[g]
        lnx_w, lnx_b = vecs[g][6], vecs[g][7]
        d = y - head_sum(y) * (1.0 / HEAD_DIM)
        var = head_sum(d * d) * (1.0 / HEAD_DIM)
        yn = d * lax.rsqrt(var + LNX_EPS) * lnx_w + lnx_b
        o_ref[:, cols] = ((yn + bonus) * (z * jax.nn.sigmoid(z))).astype(o_ref.dtype)
        s_ref[g] = s_new


def _rwkv_mix(r, k, v, wl, al, vl, v_first, z, vec_params):
    B, T, D = r.shape
    C, W = RWKV_CHUNK, MXU_DIM
    groups = next(g for g in (8, 4, 2, 1) if D % (g * W) == 0)
    spec = pl.BlockSpec((None, C, groups * W), lambda bb, g, c: (bb, c, g))
    vec_spec = pl.BlockSpec((1, groups * W), lambda bb, g, c: (0, g))
    return pl.pallas_call(
        _rwkv_kernel,
        grid=(B, D // (groups * W), T // C),
        in_specs=[spec] * _RWKV_ROW_INPUTS + [vec_spec] * _RWKV_VEC_PARAMS,
        out_specs=spec,
        out_shape=jax.ShapeDtypeStruct((B, T, D), BF16),
        scratch_shapes=[pltpu.VMEM((groups, W, W), F32)],
        compiler_params=pltpu.CompilerParams(
            dimension_semantics=("parallel", "parallel", "arbitrary"), vmem_limit_bytes=VMEM_LIMIT),
        name="rwkv7_mix",
    )(r, k, v, wl, al, vl, v_first, z, *[p.reshape(1, D) for p in vec_params])


def _pad_cols(w, n):
    return jnp.pad(w, ((0, 0), (0, n - w.shape[1])))


def _pad_rows(w, n):
    return jnp.pad(w, ((0, n - w.shape[0]), (0, 0)))


def _fox_layer(x, h, gate, w_in, b_f, q_norm_w, k_norm_w, w_out):
    B, T, D = x.shape
    H = D // HEAD_DIM
    hb = h.reshape(B * T, D)
    w_b = w_in.astype(BF16)
    qk = _matmul(hb, w_b[:, :2 * D], BF16).reshape(B, T, 2 * D)
    v, v_t = _matmul_and_transposed(hb, w_b[:, 2 * D:3 * D], B)
    v = v.reshape(B, T, D)
    z = _matmul(hb, w_b[:, 3 * D:4 * D], BF16).reshape(B, T, D)
    f_logit = _matmul(hb, _pad_cols(w_b[:, 4 * D:], LANES), F32)[:, :H].reshape(B, T, H) + b_f
    cum = jnp.cumsum(jax.nn.log_sigmoid(f_logit), axis=1)
    cum_rows = jnp.transpose(cum, (0, 2, 1)).reshape(B, H, 1, T)
    g = _fox_attention(qk, v_t, cum_rows, z, q_norm_w, k_norm_w)
    return _matmul_residual(g, w_out.astype(BF16), x, gate), v


def _rwkv_layer(x, mixes, mu_lora, gate, v_first, w_in, w0, w1, w2, a0, a1, a2, v0, v1, v2,
                k_k, k_a, r_k, lnx_w, lnx_b, w_out):
    B, T, D = x.shape
    M = B * T
    shp = (B, T, D)
    xr, xk, xv, xg, h, xx = [t.reshape(M, D) for t in mixes]
    ranks = [w1.shape[1], a1.shape[1], v1.shape[1]]
    w_h = jnp.concatenate([w1, a1, v1], axis=1)
    w_x = jnp.concatenate([mu_lora[0][:, None] * w1, mu_lora[1][:, None] * a1, mu_lora[2][:, None] * v1], axis=1)
    mid = _lora_in(h, xx, w_h.astype(BF16), w_x.astype(BF16), ranks[0])
    offs = [0, ranks[0], ranks[0] + ranks[1]]
    mids = [_pad_cols(mid[:, o:o + r], LANES) for o, r in zip(offs, ranks)]
    r = _matmul(xr, w_in[0].astype(BF16)).reshape(shp)
    k = _matmul(xk, w_in[1].astype(BF16)).reshape(shp)
    v = _matmul(xv, w_in[2].astype(BF16)).reshape(shp)
    z = _matmul(xg, w_in[3].astype(BF16), BF16).reshape(shp)
    wl, al, vl = [_matmul(m, _pad_rows(w, LANES).astype(BF16)).reshape(shp)
                  for m, w in zip(mids, (w2, a2, v2))]
    g = _rwkv_mix(r, k, v, wl, al, vl, v_first, z,
                  (w0, a0, v0, k_k, k_a, r_k.reshape(D), lnx_w, lnx_b))
    return _matmul_residual(g, w_out.astype(BF16), x, gate)


def kernel(x, c, norm_w, w_mod, b_mod, fox_w_in, fox_b_f, fox_q_norm_w, fox_k_norm_w, fox_w_out, rwkv_mu, rwkv_w_in, rwkv_w0, rwkv_w1, rwkv_w2, rwkv_a0, rwkv_a1, rwkv_a2, rwkv_v0, rwkv_v1, rwkv_v2, rwkv_k_k, rwkv_k_a, rwkv_r_k, rwkv_lnx_w, rwkv_lnx_b, rwkv_w_out):
    B, T, D = x.shape
    depth = norm_w.shape[0]
    c_pad = jnp.pad(c, ((0, (-B) % 8), (0, 0)))
    v_first = None
    for i in range(depth):
        mod = _adaln_mod(c_pad, w_mod[i], b_mod[i])[:B]
        shift, scale, gate = mod[:, :D], mod[:, D:2 * D], mod[:, 2 * D:]
        j = i // 2
        if i % 2 == 0:
            h = _norm_mod(x, norm_w[i], scale, shift)
            x, v = _fox_layer(x, h, gate, fox_w_in[j], fox_b_f[j], fox_q_norm_w[j], fox_k_norm_w[j],
                              fox_w_out[j])
            if v_first is None:
                v_first = v
        else:
            mu_r, mu_w, mu_k, mu_v, mu_a, mu_g = rwkv_mu[j]
            mixes = _norm_mod_mix(x, norm_w[i], scale, shift, jnp.stack([mu_r, mu_k, mu_v, mu_g]))
            x = _rwkv_layer(x, mixes, (mu_w, mu_a, mu_v), gate, v_first, rwkv_w_in[j], rwkv_w0[j], rwkv_w1[j],
                            rwkv_w2[j], rwkv_a0[j], rwkv_a1[j], rwkv_a2[j], rwkv_v0[j], rwkv_v1[j],
                            rwkv_v2[j], rwkv_k_k[j], rwkv_k_a[j], rwkv_r_k[j], rwkv_lnx_w[j],
                            rwkv_lnx_b[j], rwkv_w_out[j])
    return x
```

```python
import functools

import jax
import jax.numpy as jnp
from jax import lax
from jax.experimental import pallas as pl
from jax.experimental.pallas import tpu as pltpu

F32 = jnp.float32
BF16 = jnp.bfloat16

HEAD_DIM = 64
LANES = 128
MXU_DIM = 256
RWKV_CHUNK = 64
INV_SQUARINGS = 5
HEADS_PER_GROUP = MXU_DIM // HEAD_DIM
NORM_EPS = 1e-6
LNX_EPS = 64e-5
VMEM_LIMIT = 56 * 1024 * 1024
NEG_BIG = -1e30


def _tile(n, pref):
    t = min(n, pref)
    while n % t:
        t //= 2
    return t


def _norm_mod_kernel(x_ref, nw_ref, sc_ref, sh_ref, o_ref):
    x = x_ref[...]
    ms = jnp.mean(x * x, axis=-1, keepdims=True)
    y = x * lax.rsqrt(ms + NORM_EPS) * nw_ref[...]
    o_ref[...] = (y * (1.0 + sc_ref[...]) + sh_ref[...]).astype(o_ref.dtype)


def _norm_mod(x, norm_w, scale, shift):
    B, T, D = x.shape
    tm = _tile(T, 512)
    row = pl.BlockSpec((None, tm, D), lambda b, i: (b, i, 0))
    per_b = pl.BlockSpec((None, 1, D), lambda b, i: (b, 0, 0))
    return pl.pallas_call(
        _norm_mod_kernel,
        grid=(B, T // tm),
        in_specs=[row, pl.BlockSpec((1, D), lambda b, i: (0, 0)), per_b, per_b],
        out_specs=row,
        out_shape=jax.ShapeDtypeStruct((B, T, D), BF16),
        compiler_params=pltpu.CompilerParams(
            dimension_semantics=("parallel", "parallel"), vmem_limit_bytes=VMEM_LIMIT),
        name="norm_mod",
    )(x, norm_w.reshape(1, D), scale.reshape(B, 1, D), shift.reshape(B, 1, D))


def _norm_mod_mix_kernel(x_ref, xp_ref, nw_ref, sc_ref, sh_ref, mu_ref, *o_refs):
    def modulated(x):
        ms = jnp.mean(x * x, axis=-1, keepdims=True)
        return x * lax.rsqrt(ms + NORM_EPS) * nw_ref[...] * (1.0 + sc_ref[...]) + sh_ref[...]

    h = modulated(x_ref[...])
    above = modulated(xp_ref[...])[7:8, :]
    above = jnp.where(pl.program_id(1) == 0, 0.0, above)
    row = lax.broadcasted_iota(jnp.int32, h.shape, 0)
    xx = jnp.where(row == 0, above, pltpu.roll(h, 1, axis=0)) - h
    for i, o_ref in enumerate(o_refs[:-2]):
        o_ref[...] = (h + xx * mu_ref[i:i + 1, :]).astype(o_ref.dtype)
    o_refs[-2][...] = h.astype(o_refs[-2].dtype)
    o_refs[-1][...] = xx.astype(o_refs[-1].dtype)


def _norm_mod_mix(x, norm_w, scale, shift, mu):
    B, T, D = x.shape
    n = mu.shape[0] + 2
    tm = _tile(T, 512)
    row = pl.BlockSpec((None, tm, D), lambda b, i: (b, i, 0))
    prev = pl.BlockSpec((None, 8, D), lambda b, i: (b, jnp.maximum(i * (tm // 8) - 1, 0), 0))
    per_b = pl.BlockSpec((None, 1, D), lambda b, i: (b, 0, 0))
    return pl.pallas_call(
        _norm_mod_mix_kernel,
        grid=(B, T // tm),
        in_specs=[row, prev, pl.BlockSpec((1, D), lambda b, i: (0, 0)), per_b, per_b,
                  pl.BlockSpec((n - 2, D), lambda b, i: (0, 0))],
        out_specs=[row] * n,
        out_shape=[jax.ShapeDtypeStruct((B, T, D), BF16)] * n,
        compiler_params=pltpu.CompilerParams(
            dimension_semantics=("parallel", "parallel"), vmem_limit_bytes=VMEM_LIMIT),
        name="norm_mod_mix",
    )(x, x, norm_w.reshape(1, D), scale.reshape(B, 1, D), shift.reshape(B, 1, D), mu)


def _matmul_kernel(a_ref, b_ref, o_ref):
    o_ref[...] = jnp.dot(a_ref[...], b_ref[...], preferred_element_type=F32).astype(o_ref.dtype)


def _matmul(a, b, out_dtype=F32):
    M, K = a.shape
    N = b.shape[1]
    tm, tn = _tile(M, 1024), _tile(N, 1024)
    return pl.pallas_call(
        _matmul_kernel,
        grid=(M // tm, N // tn),
        in_specs=[pl.BlockSpec((tm, K), lambda i, j: (i, 0)),
                  pl.BlockSpec((K, tn), lambda i, j: (0, j))],
        out_specs=pl.BlockSpec((tm, tn), lambda i, j: (i, j)),
        out_shape=jax.ShapeDtypeStruct((M, N), out_dtype),
        compiler_params=pltpu.CompilerParams(
            dimension_semantics=("parallel", "parallel"), vmem_limit_bytes=VMEM_LIMIT),
        name="matmul",
    )(a, b)


def _matmul_t_kernel(a_ref, b_ref, o_ref, ot_ref):
    acc = jnp.dot(a_ref[...], b_ref[...], preferred_element_type=F32)
    o_ref[...] = acc
    ot_ref[...] = jnp.transpose(acc).astype(ot_ref.dtype)


def _matmul_and_transposed(a, b, batch):
    M, K = a.shape
    N = b.shape[1]
    T = M // batch
    tm, tn = _tile(T, 1024), _tile(N, 1024)
    tiles_per_batch = T // tm
    return pl.pallas_call(
        _matmul_t_kernel,
        grid=(M // tm, N // tn),
        in_specs=[pl.BlockSpec((tm, K), lambda i, j: (i, 0)),
                  pl.BlockSpec((K, tn), lambda i, j: (0, j))],
        out_specs=[pl.BlockSpec((tm, tn), lambda i, j: (i, j)),
                   pl.BlockSpec((None, tn, tm), lambda i, j: (i // tiles_per_batch, j, i % tiles_per_batch))],
        out_shape=[jax.ShapeDtypeStruct((M, N), F32), jax.ShapeDtypeStruct((batch, N, T), BF16)],
        compiler_params=pltpu.CompilerParams(
            dimension_semantics=("parallel", "parallel"), vmem_limit_bytes=VMEM_LIMIT),
        name="matmul_t",
    )(a, b)


def _lora_in_kernel(h_ref, xx_ref, wh_ref, wx_ref, o_ref, *, n_tanh):
    acc = (jnp.dot(h_ref[...], wh_ref[...], preferred_element_type=F32)
           + jnp.dot(xx_ref[...], wx_ref[...], preferred_element_type=F32))
    col = lax.broadcasted_iota(jnp.int32, acc.shape, 1)
    o_ref[...] = jnp.where(col < n_tanh, jnp.tanh(acc), acc).astype(o_ref.dtype)


def _lora_in(h, xx, w_h, w_x, n_tanh):
    M, K = h.shape
    R = w_h.shape[1]
    tm = _tile(M, 1024)
    lhs = pl.BlockSpec((tm, K), lambda i: (i, 0))
    rhs = pl.BlockSpec((K, R), lambda i: (0, 0))
    return pl.pallas_call(
        functools.partial(_lora_in_kernel, n_tanh=n_tanh),
        grid=(M // tm,),
        in_specs=[lhs, lhs, rhs, rhs],
        out_specs=pl.BlockSpec((tm, R), lambda i: (i, 0)),
        out_shape=jax.ShapeDtypeStruct((M, R), BF16),
        compiler_params=pltpu.CompilerParams(
            dimension_semantics=("parallel",), vmem_limit_bytes=VMEM_LIMIT),
        name="lora_in",
    )(h, xx, w_h, w_x)


def _matmul_residual_kernel(a_ref, b_ref, x_ref, g_ref, o_ref):
    acc = jnp.dot(a_ref[...], b_ref[...], preferred_element_type=F32)
    o_ref[...] = x_ref[...] + g_ref[...] * acc


def _matmul_residual(a, b, x, gate):
    B, T, K = a.shape
    N = b.shape[1]
    tm, tn = _tile(T, 1024), _tile(N, 1024)
    return pl.pallas_call(
        _matmul_residual_kernel,
        grid=(B, T // tm, N // tn),
        in_specs=[pl.BlockSpec((None, tm, K), lambda bb, i, j: (bb, i, 0)),
                  pl.BlockSpec((K, tn), lambda bb, i, j: (0, j)),
                  pl.BlockSpec((None, tm, tn), lambda bb, i, j: (bb, i, j)),
                  pl.BlockSpec((None, 1, tn), lambda bb, i, j: (bb, 0, j))],
        out_specs=pl.BlockSpec((None, tm, tn), lambda bb, i, j: (bb, i, j)),
        out_shape=jax.ShapeDtypeStruct((B, T, N), F32),
        compiler_params=pltpu.CompilerParams(
            dimension_semantics=("parallel", "parallel", "parallel"), vmem_limit_bytes=VMEM_LIMIT),
        name="matmul_residual",
    )(a, b, x, gate.reshape(B, 1, N))


def _mod_kernel(c_ref, w_ref, b_ref, o_ref):
    c = c_ref[...]
    c_act = c * jax.nn.sigmoid(c)
    hi = c_act.astype(BF16)
    lo = (c_act - hi.astype(F32)).astype(BF16)
    w = w_ref[...]
    w_hi = w.astype(BF16)
    w_lo = (w - w_hi.astype(F32)).astype(BF16)
    acc = (jnp.dot(lo, w_hi, preferred_element_type=F32) + jnp.dot(hi, w_lo, preferred_element_type=F32)
           + jnp.dot(hi, w_hi, preferred_element_type=F32))
    o_ref[...] = acc + b_ref[...]


def _adaln_mod(c_pad, w_mod, b_mod):
    R, D = c_pad.shape
    N = w_mod.shape[1]
    tn = _tile(N, 512)
    return pl.pallas_call(
        _mod_kernel,
        grid=(N // tn,),
        in_specs=[pl.BlockSpec((R, D), lambda j: (0, 0)),
                  pl.BlockSpec((D, tn), lambda j: (0, j)),
                  pl.BlockSpec((1, tn), lambda j: (0, j))],
        out_specs=pl.BlockSpec((R, tn), lambda j: (0, j)),
        out_shape=jax.ShapeDtypeStruct((R, N), F32),
        compiler_params=pltpu.CompilerParams(
            dimension_semantics=("parallel",), vmem_limit_bytes=VMEM_LIMIT),
        name="adaln_mod",
    )(c_pad, w_mod, b_mod.reshape(1, N))


LOG2E = 1.4426950408889634
KEY_BLOCKS_PER_TILE = 2
PV_PAD = 16


def _pair_rms(x, lo_half, w):
    ss = x * x
    s_lo = jnp.sum(jnp.where(lo_half, ss, 0.0), axis=1, keepdims=True)
    s_hi = jnp.sum(jnp.where(lo_half, 0.0, ss), axis=1, keepdims=True)
    ms = jnp.where(lo_half, s_lo, s_hi) * (1.0 / HEAD_DIM)
    return x * lax.rsqrt(ms + NORM_EPS) * w


def _fox_attn_kernel(q_ref, k_ref, vt_ref, c_ref, z_ref, qw_ref, kw_ref, o_ref, kn_ref, vat_ref, m_ref, acc_ref,
                     s_ref, *, tq, tb):
    qi = pl.program_id(2)
    T = k_ref.shape[0]

    @pl.when(qi == 0)
    def _():
        lo_b = lax.broadcasted_iota(jnp.int32, (tb, LANES), 1) < HEAD_DIM
        row_b = lax.broadcasted_iota(jnp.int32, (LANES, tb), 0)
        top_b = row_b < HEAD_DIM

        def build(cidx, carry):
            blk = pl.ds(pl.multiple_of(cidx * tb, tb), tb)
            kn_ref[blk, :] = _pair_rms(k_ref[blk, :].astype(F32), lo_b, kw_ref[...]).astype(BF16)
            vt = vt_ref[:, blk].astype(F32)
            vat_ref[0, :, blk] = jnp.where(top_b, vt, jnp.where(row_b == HEAD_DIM, 1.0, 0.0)).astype(BF16)
            vat_ref[1, :, blk] = jnp.where(top_b, jnp.where(row_b == HEAD_DIM - 1, 1.0, 0.0), vt).astype(BF16)
            return carry

        lax.fori_loop(0, T // tb, build, 0)

    lo = lax.broadcasted_iota(jnp.int32, (tq, LANES), 1) < HEAD_DIM
    qn = _pair_rms(q_ref[...].astype(F32), lo, qw_ref[...])
    qh = (jnp.where(lo, qn, 0.0).astype(BF16), jnp.where(lo, 0.0, qn).astype(BF16))
    q_start = pl.multiple_of(qi * tq, tq)
    c_ref0 = [c_ref[h, :, pl.ds(q_start, LANES)][:, 0:1] for h in range(2)]
    tk = tq // KEY_BLOCKS_PER_TILE
    row_minus_col = (lax.broadcasted_iota(jnp.int32, (tk, tq), 0)
                     - lax.broadcasted_iota(jnp.int32, (tk, tq), 1))
    nt = (((1,), (1,)), ((), ()))
    va_rows = (slice(0, HEAD_DIM + PV_PAD), slice(HEAD_DIM - PV_PAD, LANES))

    m_ref[...] = jnp.full(m_ref.shape, NEG_BIG, F32)
    acc_ref[...] = jnp.zeros(acc_ref.shape, F32)

    def scores(start, slot):
        start = pl.multiple_of(start, tk)
        kj = kn_ref[pl.ds(start, tk), :]
        for h in range(2):
            bias = (c_ref0[h] - c_ref[h, :, pl.ds(start, tk)]) * LOG2E
            bias_col = jnp.transpose(jnp.broadcast_to(bias, (8, tk)))[:, 0:1]
            s_ref[slot, h] = lax.dot_general(kj, qh[h], nt, preferred_element_type=F32) + bias_col

    def consume(start, slot, masked):
        start = pl.multiple_of(start, tk)
        for h in range(2):
            s = s_ref[slot, h]
            if masked:
                s = jnp.where(row_minus_col <= q_start - start, s, NEG_BIG)
            m = m_ref[h]
            m_new = jnp.maximum(m, jnp.max(s, axis=0, keepdims=True))
            p = jnp.exp2((s - m_new).astype(BF16))
            acc_ref[h] = jnp.exp2(m - m_new) * acc_ref[h] + jnp.dot(
                vat_ref[h, va_rows[h], pl.ds(start, tk)], p, preferred_element_type=F32)
            m_ref[h] = m_new

    scores(0, 0)

    def body(i, carry):
        base = i * (2 * tk)
        scores(base + tk, 1)
        consume(base, 0, False)
        scores(base + 2 * tk, 0)
        consume(base + tk, 1, False)
        return carry

    lax.fori_loop(0, qi * (KEY_BLOCKS_PER_TILE // 2), body, 0)
    for t in range(KEY_BLOCKS_PER_TILE):
        if t + 1 < KEY_BLOCKS_PER_TILE:
            scores(q_start + (t + 1) * tk, (t + 1) % 2)
        consume(q_start + t * tk, t % 2, True)
    acc0, acc1 = acc_ref[0], acc_ref[1]
    o_t = jnp.concatenate([acc0[:HEAD_DIM] / acc0[HEAD_DIM:HEAD_DIM + 1, :],
                           acc1[PV_PAD:] / acc1[PV_PAD - 1:PV_PAD, :]], axis=0)
    z = z_ref[...].astype(F32)
    o_ref[...] = (jnp.transpose(o_t) * (z * jax.nn.sigmoid(z))).astype(o_ref.dtype)


def _fox_attention(qk, v_t, cum_rows, z, q_norm_w, k_norm_w):
    B, D, T = v_t.shape
    H = D // HEAD_DIM
    tq = _tile(T, 1024)
    n_pair = D // LANES
    qw = jnp.tile(q_norm_w, 2).reshape(1, LANES) * (HEAD_DIM ** -0.5 * LOG2E)
    kw = jnp.tile(k_norm_w, 2).reshape(1, LANES)
    once = pl.Buffered(1)
    return pl.pallas_call(
        functools.partial(_fox_attn_kernel, tq=tq, tb=_tile(T, 512)),
        grid=(B, H // 2, T // tq),
        in_specs=[pl.BlockSpec((None, tq, LANES), lambda b, hp, i: (b, i, hp)),
                  pl.BlockSpec((None, T, LANES), lambda b, hp, i: (b, 0, n_pair + hp), pipeline_mode=once),
                  pl.BlockSpec((None, LANES, T), lambda b, hp, i: (b, hp, 0), pipeline_mode=once),
                  pl.BlockSpec((None, 2, 1, T), lambda b, hp, i: (b, hp, 0, 0), pipeline_mode=once),
                  pl.BlockSpec((None, tq, LANES), lambda b, hp, i: (b, i, hp)),
                  pl.BlockSpec((1, LANES), lambda b, hp, i: (0, 0)),
                  pl.BlockSpec((1, LANES), lambda b, hp, i: (0, 0))],
        out_specs=pl.BlockSpec((None, tq, LANES), lambda b, hp, i: (b, i, hp)),
        out_shape=jax.ShapeDtypeStruct((B, T, D), BF16),
        scratch_shapes=[pltpu.VMEM((T, LANES), BF16), pltpu.VMEM((2, LANES, T), BF16),
                        pltpu.VMEM((2, 1, tq), F32), pltpu.VMEM((2, HEAD_DIM + PV_PAD, tq), F32),
                        pltpu.VMEM((2, 2, tq // KEY_BLOCKS_PER_TILE, tq), F32)],
        compiler_params=pltpu.CompilerParams(
            dimension_semantics=("parallel", "parallel", "arbitrary"), vmem_limit_bytes=VMEM_LIMIT),
        name="fox_attention",
    )(qk, qk, v_t, cum_rows, z, qw, kw)


def _trunc_bf16(x):
    bits = lax.bitcast_convert_type(x, jnp.uint32) & jnp.uint32(0xFFFF0000)
    return lax.bitcast_convert_type(bits, F32)


def _split3(x):
    hi = _trunc_bf16(x)
    r1 = x - hi
    mid = _trunc_bf16(r1)
    lo = r1 - mid
    return hi.astype(BF16), mid.astype(BF16), lo.astype(BF16)


_NN = (((1,), (0,)), ((), ()))
_NT = (((1,), (1,)), ((), ()))
_TN = (((0,), (0,)), ((), ()))


def _hl(x):
    hi = x.astype(BF16)
    return hi, (x - hi.astype(F32)).astype(BF16)


def _h(x):
    return (x.astype(BF16),)


def _mm3(a, b, dims=_NN):
    dg = functools.partial(lax.dot_general, dimension_numbers=dims, preferred_element_type=F32)
    if len(a) == 1 or len(b) == 1:
        return dg(a[0], b[0])
    return (dg(a[1], b[0]) + dg(a[0], b[1])) + dg(a[0], b[0])


def _rwkv_chunks(groups, head_masks):
    C, W = groups[0][0].shape
    assert C == 2 ** (INV_SQUARINGS + 1) == HEAD_DIM
    n = range(len(groups))
    t_idx = lax.broadcasted_iota(jnp.int32, (C, W), 0)
    i_idx = lax.broadcasted_iota(jnp.int32, (C, W), 1) & (HEAD_DIM - 1)
    strict = i_idx < t_idx
    incl = i_idx <= t_idx
    eye = jnp.where(i_idx == t_idx, 1.0, 0.0)
    rw = lax.broadcasted_iota(jnp.int32, (W, W), 0) >> 6
    cw = lax.broadcasted_iota(jnp.int32, (W, W), 1) >> 6
    same_head = rw == cw

    def bd(parts):
        return tuple(jnp.concatenate([jnp.where(mk, p, jnp.zeros_like(p)) for mk in head_masks], axis=0)
                     for p in parts)

    rr = lax.broadcasted_iota(jnp.int32, (C, C), 0)
    cc = lax.broadcasted_iota(jnp.int32, (C, C), 1)
    tri = jnp.where(cc <= rr, 1.0, 0.0).astype(BF16)
    mm = functools.partial(jnp.dot, preferred_element_type=F32)

    r, lw, k, v, a, b, S = [[grp[i] for grp in groups] for i in range(7)]
    lw3 = [_split3(lw[g]) for g in n]
    cum = [(mm(tri, lw3[g][2]) + mm(tri, lw3[g][1])) + mm(tri, lw3[g][0]) for g in n]
    e_neg = [jnp.exp(-cum[g]) for g in n]
    AR = [_h(jnp.concatenate([a[g] * jnp.exp(cum[g] - lw[g]), r[g] * jnp.exp(cum[g])], axis=0))
          for g in n]
    sB = [_mm3(AR[g], bd(_h(b[g] * e_neg[g])), _NT) for g in n]
    sK = [_mm3(AR[g], bd(_h(k[g] * e_neg[g])), _NT) for g in n]
    A_ab = [jnp.where(strict, sB[g][:C], 0.0) for g in n]

    P = [eye + A_ab[g] for g in n]
    Apow = [_hl(A_ab[g]) for g in n]
    Apow = [_hl(_mm3(Apow[g], bd(Apow[g]))) for g in n]
    for step in range(INV_SQUARINGS):
        last = step == INV_SQUARINGS - 1
        lhs = [_hl(P[g]) if last else tuple(jnp.concatenate([p_part, a_part], axis=0)
                                            for p_part, a_part in zip(_hl(P[g]), Apow[g])) for g in n]
        prod = [_mm3(lhs[g], bd(Apow[g])) for g in n]
        P = [P[g] + prod[g][:C] for g in n]
        if not last:
            Apow = [_hl(prod[g][C:]) for g in n]

    A_k = [_h(jnp.concatenate([jnp.where(strict, sK[g][:C], 0.0), jnp.where(incl, sK[g][C:], 0.0)], axis=0))
           for g in n]
    XY = [_mm3(AR[g], _h(S[g]), _NT) + _mm3(A_k[g], bd(_h(v[g]))) for g in n]
    U = [_mm3(_hl(P[g]), bd(_hl(XY[g][:C]))) for g in n]
    Y = [XY[g][C:] + _mm3(_h(jnp.where(incl, sB[g][C:], 0.0)), bd(_h(U[g]))) for g in n]
    out = []
    for g in n:
        cum_end = cum[g][C - 1:C, :]
        e_end = jnp.exp(cum_end - cum[g])
        upd = _mm3(_h(jnp.concatenate([U[g], v[g]], axis=0)),
                   _h(jnp.concatenate([b[g] * e_end, k[g] * e_end], axis=0)), _TN)
        out.append((Y[g], S[g] * jnp.exp(cum_end) + jnp.where(same_head, upd, 0.0)))
    return out


_RWKV_ROW_INPUTS = 8
_RWKV_VEC_PARAMS = 8


def _rwkv_kernel(*refs):
    row_refs = refs[:_RWKV_ROW_INPUTS]
    vec_refs = refs[_RWKV_ROW_INPUTS:_RWKV_ROW_INPUTS + _RWKV_VEC_PARAMS]
    o_ref, s_ref = refs[_RWKV_ROW_INPUTS + _RWKV_VEC_PARAMS:]

    @pl.when(pl.program_id(2) == 0)
    def _():
        s_ref[...] = jnp.zeros_like(s_ref)

    C, W = row_refs[0].shape[0], MXU_DIM
    lane_head = lax.broadcasted_iota(jnp.int32, (C, W), 1) >> 6
    in_head = [lane_head == h for h in range(HEADS_PER_GROUP)]
    col_slices = [slice(g * W, (g + 1) * W) for g in range(row_refs[0].shape[1] // W)]
    n = range(len(col_slices))

    rw = lax.broadcasted_iota(jnp.int32, (W, W), 0) >> 6
    cw = lax.broadcasted_iota(jnp.int32, (W, W), 1) >> 6
    head_ones = jnp.where(rw == cw, 1.0, 0.0).astype(BF16)

    def head_sums(xs):
        hi, lo = _hl(jnp.concatenate(xs, axis=0))
        tot = (jnp.dot(lo, head_ones, preferred_element_type=F32)
               + jnp.dot(hi, head_ones, preferred_element_type=F32))
        return [tot[i * C:(i + 1) * C] for i in range(len(xs))]

    rows = [[ref[:, cols].astype(F32) for ref in row_refs] for cols in col_slices]
    vecs = [[ref[:, cols] for ref in vec_refs] for cols in col_slices]
    rate = [jax.nn.sigmoid(vecs[g][1] + rows[g][4]) for g in n]
    kk = [rows[g][1] * vecs[g][3] for g in n]
    k2 = [rows[g][1] * (1.0 + (rate[g] - 1.0) * vecs[g][4]) for g in n]
    sums = head_sums([kk[g] * kk[g] for g in n] + [rows[g][0] * k2[g] * vecs[g][5] for g in n])
    groups, keep = [], []
    for g in n:
        r, k, v, wl, al, vl, vf, z = rows[g]
        w0, v0 = vecs[g][0], vecs[g][2]
        t = -(w0 + wl)
        softplus = jnp.maximum(t, 0.0) + jnp.log(1.0 + jnp.exp(-jnp.abs(t)))
        lw = -jnp.exp(-softplus - 0.5)
        vv = v + (vf - v) * jax.nn.sigmoid(v0 + vl)
        kkn = kk[g] / jnp.maximum(jnp.sqrt(sums[g]), 1e-12)
        groups.append((r, lw, k2[g], vv, -kkn, kkn * rate[g], s_ref[g]))
        keep.append((sums[len(n) + g] * vv, z))
    results = _rwkv_chunks(groups, in_head)
    means = head_sums([results[g][0] for g in n])
    d = [results[g][0] - means[g] * (1.0 / HEAD_DIM) for g in n]
    var = head_sums([d[g] * d[g] for g in n])
    for g, cols in enumerate(col_slices):
        bonus, z = keep[g]
        lnx_w, lnx_b = vecs[g][6], vecs[g][7]
        yn = d[g] * lax.rsqrt(var[g] * (1.0 / HEAD_DIM) + LNX_EPS) * lnx_w + lnx_b
        o_ref[:, cols] = ((yn + bonus) * (z * jax.nn.sigmoid(z))).astype(o_ref.dtype)
        s_ref[g] = results[g][1]


def _rwkv_mix(r, k, v, wl, al, vl, v_first, z, vec_params):
    B, T, D = r.shape
    C, W = RWKV_CHUNK, MXU_DIM
    groups = next(g for g in (8, 4, 2, 1) if D % (g * W) == 0)
    spec = pl.BlockSpec((None, C, groups * W), lambda bb, g, c: (bb, c, g))
    vec_spec = pl.BlockSpec((1, groups * W), lambda bb, g, c: (0, g))
    return pl.pallas_call(
        _rwkv_kernel,
        grid=(B, D // (groups * W), T // C),
        in_specs=[spec] * _RWKV_ROW_INPUTS + [vec_spec] * _RWKV_VEC_PARAMS,
        out_specs=spec,
        out_shape=jax.ShapeDtypeStruct((B, T, D), BF16),
        scratch_shapes=[pltpu.VMEM((groups, W, W), F32)],
        compiler_params=pltpu.CompilerParams(
            dimension_semantics=("parallel", "parallel", "arbitrary"), vmem_limit_bytes=VMEM_LIMIT),
        name="rwkv7_mix",
    )(r, k, v, wl, al, vl, v_first, z, *[p.reshape(1, D) for p in vec_params])


def _pad_cols(w, n):
    return jnp.pad(w, ((0, 0), (0, n - w.shape[1])))


def _pad_rows(w, n):
    return jnp.pad(w, ((0, n - w.shape[0]), (0, 0)))


def _fox_layer(x, h, gate, w_in, b_f, q_norm_w, k_norm_w, w_out):
    B, T, D = x.shape
    H = D // HEAD_DIM
    hb = h.reshape(B * T, D)
    w_b = w_in.astype(BF16)
    qk = _matmul(hb, w_b[:, :2 * D], BF16).reshape(B, T, 2 * D)
    v, v_t = _matmul_and_transposed(hb, w_b[:, 2 * D:3 * D], B)
    v = v.reshape(B, T, D)
    z = _matmul(hb, w_b[:, 3 * D:4 * D], BF16).reshape(B, T, D)
    f_logit = _matmul(hb, _pad_cols(w_b[:, 4 * D:], LANES), F32)[:, :H].reshape(B, T, H) + b_f
    cum = jnp.cumsum(jax.nn.log_sigmoid(f_logit), axis=1)
    cum_rows = jnp.transpose(cum, (0, 2, 1)).reshape(B, H, 1, T)
    g = _fox_attention(qk, v_t, cum_rows, z, q_norm_w, k_norm_w)
    return _matmul_residual(g, w_out.astype(BF16), x, gate), v


def _rwkv_layer(x, mixes, mu_lora, gate, v_first, w_in, w0, w1, w2, a0, a1, a2, v0, v1, v2,
                k_k, k_a, r_k, lnx_w, lnx_b, w_out):
    B, T, D = x.shape
    M = B * T
    shp = (B, T, D)
    xr, xk, xv, xg, h, xx = [t.reshape(M, D) for t in mixes]
    ranks = [w1.shape[1], a1.shape[1], v1.shape[1]]
    w_h = jnp.concatenate([w1, a1, v1], axis=1)
    w_x = jnp.concatenate([mu_lora[0][:, None] * w1, mu_lora[1][:, None] * a1, mu_lora[2][:, None] * v1], axis=1)
    mid = _lora_in(h, xx, w_h.astype(BF16), w_x.astype(BF16), ranks[0])
    offs = [0, ranks[0], ranks[0] + ranks[1]]
    mids = [_pad_cols(mid[:, o:o + r], LANES) for o, r in zip(offs, ranks)]
    r = _matmul(xr, w_in[0].astype(BF16)).reshape(shp)
    k = _matmul(xk, w_in[1].astype(BF16)).reshape(shp)
    v = _matmul(xv, w_in[2].astype(BF16)).reshape(shp)
    z = _matmul(xg, w_in[3].astype(BF16), BF16).reshape(shp)
    wl, al, vl = [_matmul(m, _pad_rows(w, LANES).astype(BF16)).reshape(shp)
                  for m, w in zip(mids, (w2, a2, v2))]
    g = _rwkv_mix(r, k, v, wl, al, vl, v_first, z,
                  (w0, a0, v0, k_k, k_a, r_k.reshape(D), lnx_w, lnx_b))
    return _matmul_residual(g, w_out.astype(BF16), x, gate)


def kernel(x, c, norm_w, w_mod, b_mod, fox_w_in, fox_b_f, fox_q_norm_w, fox_k_norm_w, fox_w_out, rwkv_mu, rwkv_w_in, rwkv_w0, rwkv_w1, rwkv_w2, rwkv_a0, rwkv_a1, rwkv_a2, rwkv_v0, rwkv_v1, rwkv_v2, rwkv_k_k, rwkv_k_a, rwkv_r_k, rwkv_lnx_w, rwkv_lnx_b, rwkv_w_out):
    B, T, D = x.shape
    depth = norm_w.shape[0]
    c_pad = jnp.pad(c, ((0, (-B) % 8), (0, 0)))
    v_first = None
    for i in range(depth):
        mod = _adaln_mod(c_pad, w_mod[i], b_mod[i])[:B]
        shift, scale, gate = mod[:, :D], mod[:, D:2 * D], mod[:, 2 * D:]
        j = i // 2
        if i % 2 == 0:
            h = _norm_mod(x, norm_w[i], scale, shift)
            x, v = _fox_layer(x, h, gate, fox_w_in[j], fox_b_f[j], fox_q_norm_w[j], fox_k_norm_w[j],
                              fox_w_out[j])
            if v_first is None:
                v_first = v
        else:
            mu_r, mu_w, mu_k, mu_v, mu_a, mu_g = rwkv_mu[j]
            mixes = _norm_mod_mix(x, norm_w[i], scale, shift, jnp.stack([mu_r, mu_k, mu_v, mu_g]))
            x = _rwkv_layer(x, mixes, (mu_w, mu_a, mu_v), gate, v_first, rwkv_w_in[j], rwkv_w0[j], rwkv_w1[j],
                            rwkv_w2[j], rwkv_a0[j], rwkv_a1[j], rwkv_a2[j], rwkv_v0[j], rwkv_v1[j],
                            rwkv_v2[j], rwkv_k_k[j], rwkv_k_a[j], rwkv_r_k[j], rwkv_lnx_w[j],
                            rwkv_lnx_b[j], rwkv_w_out[j])
    return x
```
